```python
import math, functools
import jax, jax.numpy as jnp
from jax import lax
import numpy as np

D_MODEL = 1024
BATCH = 8
SEQ = 2048
DEPTH = 1
DEC_BATCH = 32
DEC_SEQ = 1
PAST_LEN = 16384
PAGE_SIZE = 128

MIX_WIDTH = D_MODEL
SB_WIDTH = MIX_WIDTH // 2
SSM_WIDTH = MIX_WIDTH - SB_WIDTH
SB_HEAD_DIM = 64
SB_HEADS = SB_WIDTH // SB_HEAD_DIM
SB_BIAS_INIT = -8.0
SSM_GROUP = 16
SSM_GROUPS = SSM_WIDTH // SSM_GROUP
SSM_STATE = 64
D_FF = 2816
N_MEM = 256
X_HEADS = 4
X_HEAD_DIM = D_MODEL // X_HEADS
Q_BLOCK = 128
EPS = 1e-6
DT_MIN = 0.001
DT_MAX = 0.1
A_RE_CLIP = -1e-4

kernel_name = 'hymba_stickbreak_s5_macaron_step'

F32 = jnp.float32


def rmsnorm(x, g):
    xf = x.astype(F32)
    y = xf * lax.rsqrt(jnp.mean(xf * xf, axis=-1, keepdims=True) + EPS)
    return (y * g.astype(F32)).astype(x.dtype)


def half_ffn(x, g, w_gate, w_up, w_down):
    h = rmsnorm(x, g)
    return x + 0.5 * ((jax.nn.silu(h @ w_gate) * (h @ w_up)) @ w_down)


def sb_weights(z, q_pos, k_pos):
    mask = k_pos[None, :] < q_pos[:, None]
    log_keep = jnp.where(mask, jax.nn.log_sigmoid(-z), 0.0)
    after = lax.cumsum(log_keep, axis=z.ndim - 1, reverse=True) - log_keep
    return jnp.where(mask, jnp.exp(jax.nn.log_sigmoid(z) + after), 0.0)


def sb_prompt(q, k, v, bias):
    b, l, h, d = q.shape
    nb = l // Q_BLOCK
    scale = d ** -0.5
    k_pos = jnp.arange(l)
    bias_f = bias.astype(F32)[None, :, None, None]
    qb = q.reshape(b, nb, Q_BLOCK, h, d).transpose(1, 0, 2, 3, 4)
    qpos = jnp.arange(l).reshape(nb, Q_BLOCK)

    def block(args):
        q_blk, q_p = args
        z = jnp.einsum('bqhd,bkhd->bhqk', q_blk, k).astype(F32) * scale + bias_f
        w = sb_weights(z, q_p, k_pos)
        return jnp.einsum('bhqk,bkhd->bqhd', w.astype(v.dtype), v)

    o = lax.map(block, (qb, qpos))
    return o.transpose(1, 0, 2, 3, 4).reshape(b, l, h * d)


def sb_sample(q, k_new, v_new, bias, k_past, v_past):
    b, l, h, d = q.shape
    past = k_past.shape[1]
    scale = d ** -0.5
    q_pos = past + jnp.arange(l)
    k_pos = jnp.arange(past + l)
    bias_f = bias.astype(F32)[None, :, None, None]
    z = jnp.concatenate([jnp.einsum('bqhd,bkhd->bhqk', q, k_past),
                         jnp.einsum('bqhd,bkhd->bhqk', q, k_new)], axis=-1).astype(F32) * scale + bias_f
    w = sb_weights(z, q_pos, k_pos).astype(v_new.dtype)
    o = (jnp.einsum('bhqk,bkhd->bqhd', w[..., :past], v_past)
         + jnp.einsum('bhqk,bkhd->bqhd', w[..., past:], v_new))
    return o.reshape(b, l, h * d)


def s5_discretize(a_re, a_im, log_dt, b_re, b_im):
    dt = jnp.exp(log_dt.astype(F32))[:, None]
    lam_re = jnp.minimum(a_re.astype(F32), A_RE_CLIP)
    lam_im = a_im.astype(F32)
    mag = jnp.exp(lam_re * dt)
    ab_re = mag * jnp.cos(lam_im * dt)
    ab_im = mag * jnp.sin(lam_im * dt)
    den = lam_re * lam_re + lam_im * lam_im
    nr = ab_re - 1.0
    f_re = (nr * lam_re + ab_im * lam_im) / den
    f_im = (ab_im * lam_re - nr * lam_im) / den
    br = b_re.astype(F32)
    bi = b_im.astype(F32)
    bb_re = f_re[..., None] * br - f_im[..., None] * bi
    bb_im = f_re[..., None] * bi + f_im[..., None] * br
    return ab_re, ab_im, bb_re, bb_im


def _complex_affine_combine(e1, e2):
    a1r, a1i, b1r, b1i = e1
    a2r, a2i, b2r, b2i = e2
    return (a2r * a1r - a2i * a1i, a2r * a1i + a2i * a1r,
            a2r * b1r - a2i * b1i + b2r, a2r * b1i + a2i * b1r + b2i)


def s5_mix(u, s0_re, s0_im, a_re, a_im, log_dt, b_re, b_im, c_re, c_im, d_skip, w_glu, b_glu):
    bsz, l, _ = u.shape
    uf = u.astype(F32)
    ug = uf.reshape(bsz, l, SSM_GROUPS, SSM_GROUP)
    ab_re, ab_im, bb_re, bb_im = s5_discretize(a_re, a_im, log_dt, b_re, b_im)
    bu_re = jnp.einsum('blgc,gpc->blgp', ug, bb_re)
    bu_im = jnp.einsum('blgc,gpc->blgp', ug, bb_im)
    a_r = jnp.broadcast_to(ab_re, bu_re.shape)
    a_i = jnp.broadcast_to(ab_im, bu_im.shape)
    p_re, p_im, s_re, s_im = lax.associative_scan(
        _complex_affine_combine, (a_r, a_i, bu_re, bu_im), axis=1)
    if s0_re is not None:
        s0r = s0_re.astype(F32)[:, None]
        s0i = s0_im.astype(F32)[:, None]
        s_re, s_im = (s_re + p_re * s0r - p_im * s0i,
                      s_im + p_re * s0i + p_im * s0r)
    y = (jnp.einsum('blgp,gcp->blgc', s_re, c_re.astype(F32))
         - jnp.einsum('blgp,gcp->blgc', s_im, c_im.astype(F32)))
    y = y.reshape(bsz, l, SSM_WIDTH) + d_skip.astype(F32) * uf
    g = jax.nn.gelu(y)
    out = g * jax.nn.sigmoid(g @ w_glu.astype(F32) + b_glu.astype(F32))
    return out.astype(u.dtype), s_re[:, -1], s_im[:, -1]


def memory_kv(mem, g_mem, w_xk, w_xv):
    b, m, _ = mem.shape
    hm = rmsnorm(mem, g_mem)
    return ((hm @ w_xk).reshape(b, m, X_HEADS, X_HEAD_DIM),
            (hm @ w_xv).reshape(b, m, X_HEADS, X_HEAD_DIM))


def cross_attend(h, w_xq, w_xo, mem_k, mem_v):
    b, l, _ = h.shape
    q = (h @ w_xq).reshape(b, l, X_HEADS, X_HEAD_DIM)
    s = jnp.einsum('blhd,bmhd->bhlm', q, mem_k).astype(F32) * (X_HEAD_DIM ** -0.5)
    p = jax.nn.softmax(s, axis=-1).astype(mem_v.dtype)
    o = jnp.einsum('bhlm,bmhd->blhd', p, mem_v).reshape(b, l, D_MODEL)
    return o @ w_xo


def layer_step(x, lp, sb_fn, s0_re, s0_im, mem_k, mem_v):
    b, l, _ = x.shape
    x = half_ffn(x, lp['g_ffn1'], lp['w_ffn1_gate'], lp['w_ffn1_up'], lp['w_ffn1_down'])
    h = rmsnorm(x, lp['g_mix'])
    proj = h @ lp['w_in']
    q, k, v, u = jnp.split(proj, [SB_WIDTH, 2 * SB_WIDTH, 3 * SB_WIDTH], axis=-1)
    q = q.reshape(b, l, SB_HEADS, SB_HEAD_DIM)
    k = k.reshape(b, l, SB_HEADS, SB_HEAD_DIM)
    v = v.reshape(b, l, SB_HEADS, SB_HEAD_DIM)
    o_sb = sb_fn(q, k, v, lp['sb_bias'])
    o_ssm, s_re, s_im = s5_mix(u, s0_re, s0_im, lp['ssm_a_re'], lp['ssm_a_im'], lp['ssm_log_dt'],
                               lp['ssm_b_re'], lp['ssm_b_im'], lp['ssm_c_re'], lp['ssm_c_im'],
                               lp['ssm_d'], lp['w_glu'], lp['b_glu'])
    mixed = jnp.concatenate([rmsnorm(o_sb, lp['g_sb_out']), rmsnorm(o_ssm, lp['g_ssm_out'])], axis=-1)
    x = x + mixed @ lp['w_out']
    x = x + cross_attend(rmsnorm(x, lp['g_xattn']), lp['w_xq'], lp['w_xo'], mem_k, mem_v)
    x = half_ffn(x, lp['g_ffn2'], lp['w_ffn2_gate'], lp['w_ffn2_up'], lp['w_ffn2_down'])
    return x, k, v, s_re, s_im


def setup_inputs(seed: int = 0) -> dict:
    key = jax.random.key(seed)
    ks = iter(jax.random.split(key, 48))

    def nrm(shape, scale):
        return jax.random.normal(next(ks), shape, F32) * scale

    def gain(shape):
        return 1.0 + 0.02 * jax.random.normal(next(ks), shape, F32)

    L = DEPTH
    n_pages = PAST_LEN // PAGE_SIZE
    n_used = DEC_BATCH * n_pages
    n_pool = n_used + max(1, n_used // 4)
    dm = D_MODEL ** -0.5
    inp = {}
    inp['x_prompt'] = nrm((BATCH, SEQ, D_MODEL), 1.0)
    inp['x_sample'] = nrm((DEC_BATCH, DEC_SEQ, D_MODEL), 1.0)
    inp['cache_k'] = nrm((L, n_pool, PAGE_SIZE, SB_HEADS, SB_HEAD_DIM), 1.0)
    inp['cache_v'] = nrm((L, n_pool, PAGE_SIZE, SB_HEADS, SB_HEAD_DIM), 1.0)
    inp['state_ssm_re'] = nrm((L, DEC_BATCH, SSM_GROUPS, SSM_STATE), 0.5)
    inp['state_ssm_im'] = nrm((L, DEC_BATCH, SSM_GROUPS, SSM_STATE), 0.5)
    inp['cache_mem_k'] = nrm((L, DEC_BATCH, N_MEM, X_HEADS, X_HEAD_DIM), 1.0)
    inp['cache_mem_v'] = nrm((L, DEC_BATCH, N_MEM, X_HEADS, X_HEAD_DIM), 1.0)
    inp['page_table'] = jax.random.permutation(next(ks), n_pool)[:n_used].reshape(
        DEC_BATCH, n_pages).astype(jnp.int32)
    inp['mem_prompt'] = nrm((BATCH, N_MEM, D_MODEL), 1.0)
    inp['g_ffn1'] = gain((L, D_MODEL))
    inp['w_ffn1_gate'] = nrm((L, D_MODEL, D_FF), dm)
    inp['w_ffn1_up'] = nrm((L, D_MODEL, D_FF), dm)
    inp['w_ffn1_down'] = nrm((L, D_FF, D_MODEL), D_FF ** -0.5)
    inp['g_mix'] = gain((L, D_MODEL))
    inp['w_in'] = nrm((L, D_MODEL, 3 * SB_WIDTH + SSM_WIDTH), dm)
    inp['sb_bias'] = SB_BIAS_INIT + nrm((L, SB_HEADS), 0.1)
    inp['ssm_a_re'] = -0.5 + nrm((L, SSM_GROUPS, SSM_STATE), 0.01)
    inp['ssm_a_im'] = (jnp.pi * jnp.arange(SSM_STATE, dtype=F32))[None, None, :] + nrm(
        (L, SSM_GROUPS, SSM_STATE), 0.01)
    inp['ssm_log_dt'] = jax.random.uniform(next(ks), (L, SSM_GROUPS), F32,
                                           math.log(DT_MIN), math.log(DT_MAX))
    inp['ssm_b_re'] = nrm((L, SSM_GROUPS, SSM_STATE, SSM_GROUP), (2 * SSM_GROUP) ** -0.5)
    inp['ssm_b_im'] = nrm((L, SSM_GROUPS, SSM_STATE, SSM_GROUP), (2 * SSM_GROUP) ** -0.5)
    inp['ssm_c_re'] = nrm((L, SSM_GROUPS, SSM_GROUP, SSM_STATE), (2 * SSM_STATE) ** -0.5)
    inp['ssm_c_im'] = nrm((L, SSM_GROUPS, SSM_GROUP, SSM_STATE), (2 * SSM_STATE) ** -0.5)
    inp['ssm_d'] = nrm((L, SSM_WIDTH), 1.0)
    inp['w_glu'] = nrm((L, SSM_WIDTH, SSM_WIDTH), SSM_WIDTH ** -0.5)
    inp['b_glu'] = nrm((L, SSM_WIDTH), 0.01)
    inp['g_sb_out'] = gain((L, SB_WIDTH))
    inp['g_ssm_out'] = gain((L, SSM_WIDTH))
    inp['w_out'] = nrm((L, MIX_WIDTH, D_MODEL), MIX_WIDTH ** -0.5)
    inp['g_xattn'] = gain((L, D_MODEL))
    inp['g_mem'] = gain((L, D_MODEL))
    inp['w_xq'] = nrm((L, D_MODEL, D_MODEL), dm)
    inp['w_xk'] = nrm((L, D_MODEL, D_MODEL), dm)
    inp['w_xv'] = nrm((L, D_MODEL, D_MODEL), dm)
    inp['w_xo'] = nrm((L, D_MODEL, D_MODEL), dm)
    inp['g_ffn2'] = gain((L, D_MODEL))
    inp['w_ffn2_gate'] = nrm((L, D_MODEL, D_FF), dm)
    inp['w_ffn2_up'] = nrm((L, D_MODEL, D_FF), dm)
    inp['w_ffn2_down'] = nrm((L, D_FF, D_MODEL), D_FF ** -0.5)
    inp['g_final'] = gain((D_MODEL,))
    return inp


def reference(x_prompt, x_sample, cache_k, cache_v, state_ssm_re, state_ssm_im,
              cache_mem_k, cache_mem_v, page_table, mem_prompt,
              g_ffn1, w_ffn1_gate, w_ffn1_up, w_ffn1_down, g_mix, w_in, sb_bias,
              ssm_a_re, ssm_a_im, ssm_log_dt, ssm_b_re, ssm_b_im, ssm_c_re, ssm_c_im,
              ssm_d, w_glu, b_glu, g_sb_out, g_ssm_out, w_out,
              g_xattn, g_mem, w_xq, w_xk, w_xv, w_xo,
              g_ffn2, w_ffn2_gate, w_ffn2_up, w_ffn2_down, g_final):
    dec_b = page_table.shape[0]
    yp, ys = x_prompt, x_sample
    kp_l, vp_l, ks_l, vs_l = [], [], [], []
    srp_l, sip_l, srs_l, sis_l = [], [], [], []
    mkp_l, mvp_l = [], []
    for l in range(DEPTH):
        lp = dict(g_ffn1=g_ffn1[l], w_ffn1_gate=w_ffn1_gate[l], w_ffn1_up=w_ffn1_up[l],
                  w_ffn1_down=w_ffn1_down[l], g_mix=g_mix[l], w_in=w_in[l], sb_bias=sb_bias[l],
                  ssm_a_re=ssm_a_re[l], ssm_a_im=ssm_a_im[l], ssm_log_dt=ssm_log_dt[l],
                  ssm_b_re=ssm_b_re[l], ssm_b_im=ssm_b_im[l], ssm_c_re=ssm_c_re[l],
                  ssm_c_im=ssm_c_im[l], ssm_d=ssm_d[l], w_glu=w_glu[l], b_glu=b_glu[l],
                  g_sb_out=g_sb_out[l], g_ssm_out=g_ssm_out[l], w_out=w_out[l],
                  g_xattn=g_xattn[l], w_xq=w_xq[l], w_xo=w_xo[l],
                  g_ffn2=g_ffn2[l], w_ffn2_gate=w_ffn2_gate[l], w_ffn2_up=w_ffn2_up[l],
                  w_ffn2_down=w_ffn2_down[l])
        mk_p, mv_p = memory_kv(mem_prompt, g_mem[l], w_xk[l], w_xv[l])
        yp, kp, vp, srp, sip = layer_step(yp, lp, sb_prompt, None, None, mk_p, mv_p)
        k_past = cache_k[l][page_table].reshape(dec_b, -1, SB_HEADS, SB_HEAD_DIM)
        v_past = cache_v[l][page_table].reshape(dec_b, -1, SB_HEADS, SB_HEAD_DIM)
        sb_fn = functools.partial(sb_sample, k_past=k_past, v_past=v_past)
        ys, ksm, vsm, srs, sis = layer_step(ys, lp, sb_fn, state_ssm_re[l], state_ssm_im[l],
                                            cache_mem_k[l], cache_mem_v[l])
        kp_l.append(kp); vp_l.append(vp); ks_l.append(ksm); vs_l.append(vsm)
        srp_l.append(srp); sip_l.append(sip); srs_l.append(srs); sis_l.append(sis)
        mkp_l.append(mk_p); mvp_l.append(mv_p)
    y_prompt = rmsnorm(yp, g_final)
    y_sample = rmsnorm(ys, g_final)
    return (y_prompt, y_sample,
            jnp.stack(kp_l), jnp.stack(vp_l), jnp.stack(ks_l), jnp.stack(vs_l),
            jnp.stack(srp_l), jnp.stack(sip_l), jnp.stack(srs_l), jnp.stack(sis_l),
            jnp.stack(mkp_l), jnp.stack(mvp_l))
```

```python
import functools
import math

import jax
import jax.numpy as jnp
from jax import lax
from jax.experimental import pallas as pl
from jax.experimental.pallas import tpu as pltpu

F32 = jnp.float32
BF16 = jnp.bfloat16

EPS = 1e-6
A_RE_CLIP = -1e-4
SB_HEAD_DIM = 64
SSM_GROUP = 16
X_HEADS = 4

LANES = 128
MXU_DIM = 256
VMEM_LIMIT_BYTES = 56 * 1024 * 1024

PROMPT_TOKEN_TILE = 256
SB_BLOCK = 256
SSM_CHUNK_STEPS = 64
PAGES_PER_STEP = 8
XATTN_Q_TILE = 512


def _params(*semantics):
    return pltpu.CompilerParams(dimension_semantics=semantics, vmem_limit_bytes=VMEM_LIMIT_BYTES)


def _const_spec(shape):
    return pl.BlockSpec(shape, lambda *_: (0,) * len(shape), pipeline_mode=pl.Buffered(1))


def _dot(a, b):
    return jnp.dot(a, b, preferred_element_type=F32)


def _dot_nt(a, b):
    return lax.dot_general(a, b, (((1,), (1,)), ((), ())), preferred_element_type=F32)


def _rms(x, g):
    return x * lax.rsqrt(jnp.mean(x * x, axis=-1, keepdims=True) + EPS) * g


def _ffn_half(x, g_ref, wg_ref, wu_ref, wd_ref):
    h = _rms(x, g_ref[...]).astype(BF16)
    gate = _dot(h, wg_ref[...])
    up = _dot(h, wu_ref[...])
    act = (gate * jax.nn.sigmoid(gate) * up).astype(BF16)
    return x + 0.5 * _dot(act, wd_ref[...])


def _log_sigmoid_pair(z):
    ls = jnp.minimum(z, 0.0) - jnp.log1p(jnp.exp(-jnp.abs(z)))
    return ls, ls - z


def _split_bf16(x):
    hi = x.astype(BF16)
    lo = (x - hi.astype(F32)).astype(BF16)
    return jnp.concatenate([hi, lo], axis=1)


def _suffix_matrix(n):
    j = lax.broadcasted_iota(jnp.int32, (2 * n, n), 0)
    s = lax.broadcasted_iota(jnp.int32, (2 * n, n), 1)
    j = jnp.where(j >= n, j - n, j)
    return jnp.where(j > s, 1.0, 0.0).astype(BF16)


def _in_proj_kernel(x_ref, g1_ref, wg_ref, wu_ref, wd_ref, gmix_ref, win_ref,
                    x1_ref, k_ref, v_ref, qb_ref, kb_ref, vb_ref, u_ref, *, sb_width):
    x1 = _ffn_half(x_ref[...], g1_ref, wg_ref, wu_ref, wd_ref)
    x1_ref[...] = x1
    h = _rms(x1, gmix_ref[...]).astype(BF16)
    proj = _dot(h, win_ref[...])
    q = proj[:, :sb_width]
    k = proj[:, sb_width:2 * sb_width]
    v = proj[:, 2 * sb_width:3 * sb_width]
    k_ref[...] = k
    v_ref[...] = v
    qb_ref[...] = (q * SB_HEAD_DIM ** -0.5).astype(BF16)
    kb_ref[...] = k.astype(BF16)
    vb_ref[...] = v.astype(BF16)
    u_ref[...] = proj[:, 3 * sb_width:]


def _in_proj(x, g1, wg, wu, wd, gmix, win, *, tm, sb_width, u_shape, u_index):
    n, d = x.shape
    dff = wg.shape[1]
    ssm_width = win.shape[1] - 3 * sb_width
    row = lambda w: pl.BlockSpec((tm, w), lambda i: (i, 0))
    return pl.pallas_call(
        functools.partial(_in_proj_kernel, sb_width=sb_width),
        grid=(n // tm,),
        in_specs=[row(d), _const_spec((1, d)), _const_spec((d, dff)), _const_spec((d, dff)),
                  _const_spec((dff, d)), _const_spec((1, d)), _const_spec(win.shape)],
        out_specs=[row(d), row(sb_width), row(sb_width), row(sb_width), row(sb_width), row(sb_width),
                   pl.BlockSpec((tm, ssm_width), u_index)],
        out_shape=[jax.ShapeDtypeStruct((n, d), F32),
                   jax.ShapeDtypeStruct((n, sb_width), F32), jax.ShapeDtypeStruct((n, sb_width), F32),
                   jax.ShapeDtypeStruct((n, sb_width), BF16), jax.ShapeDtypeStruct((n, sb_width), BF16),
                   jax.ShapeDtypeStruct((n, sb_width), BF16),
                   jax.ShapeDtypeStruct(u_shape, F32)],
        compiler_params=_params("parallel"),
        name="in_proj",
    )(x, g1, wg, wu, wd, gmix, win)


def _sb_head_block(qm, bias, kblk, vblk, suffix, carry, acc, visible):
    z = _dot_nt(qm, kblk) + bias
    ls, lk = _log_sigmoid_pair(z)
    if visible is not None:
        lk = jnp.where(visible, lk, 0.0)
    after = _dot(_split_bf16(lk), suffix) + carry
    w = jnp.exp(ls + after)
    if visible is not None:
        w = jnp.where(visible, w, 0.0)
    acc = acc + _dot(w.astype(BF16), vblk)
    carry = carry + jnp.sum(lk, axis=-1, keepdims=True)
    return carry, acc


def _sb_prompt_kernel(bias_ref, q_ref, k_ref, v_ref, o_ref, *, blk):
    pair = pl.program_id(1)
    qi = pl.program_id(2)
    q = q_ref[0]
    lane = lax.broadcasted_iota(jnp.int32, q.shape, 1)
    first = lane < SB_HEAD_DIM
    zero = jnp.zeros_like(q)
    q_heads = (jnp.where(first, q, zero), jnp.where(first, zero, q))
    biases = (bias_ref[2 * pair], bias_ref[2 * pair + 1])
    suffix = _suffix_matrix(blk)

    def key_block(j, state, visible):
        off = pl.multiple_of(j * blk, blk)
        kblk = k_ref[0, pl.ds(off, blk), :]
        vblk = v_ref[0, pl.ds(off, blk), :]
        out = []
        for h in range(2):
            out.extend(_sb_head_block(q_heads[h], biases[h], kblk, vblk, suffix,
                                      state[2 * h], state[2 * h + 1], visible))
        return tuple(out)

    state = (jnp.zeros((blk, 1), F32), jnp.zeros((blk, LANES), F32)) * 2
    qpos = lax.broadcasted_iota(jnp.int32, (blk, blk), 0)
    kpos = lax.broadcasted_iota(jnp.int32, (blk, blk), 1)
    state = key_block(qi, state, kpos < qpos)
    state = lax.fori_loop(0, qi, lambda i, st: key_block(qi - 1 - i, st, None), state)
    o_ref[0] = jnp.where(lane < SB_HEAD_DIM, state[1], state[3])


def _sb_prompt(bias, qb, kb, vb, *, blk):
    b, l, w = qb.shape
    return pl.pallas_call(
        functools.partial(_sb_prompt_kernel, blk=blk),
        grid=(b, w // LANES, l // blk),
        in_specs=[pl.BlockSpec(memory_space=pltpu.SMEM),
                  pl.BlockSpec((1, blk, LANES), lambda bi, p, qi: (bi, qi, p)),
                  pl.BlockSpec((1, l, LANES), lambda bi, p, qi: (bi, 0, p)),
                  pl.BlockSpec((1, l, LANES), lambda bi, p, qi: (bi, 0, p))],
        out_specs=pl.BlockSpec((1, blk, LANES), lambda bi, p, qi: (bi, qi, p)),
        out_shape=jax.ShapeDtypeStruct((b, l, w), F32),
        compiler_params=_params("parallel", "parallel", "arbitrary"),
        name="sb_prompt",
    )(bias, qb, kb, vb)


def _sb_sample_kernel(pt_ref, qbd_ref, knew_ref, vnew_ref, bias_ref, *refs, pps, page, new_key_visible):
    k_refs, v_refs = refs[:pps], refs[pps:2 * pps]
    o_ref, kbf, vbf, acc, carry = refs[2 * pps:]
    step = pl.program_id(1)
    qbd = qbd_ref[0]
    bias = bias_ref[...]

    @pl.when(step == 0)
    def _():
        z_new = jnp.sum(qbd.astype(F32) * knew_ref[0], axis=-1, keepdims=True) + bias
        ls_new, lk_new = _log_sigmoid_pair(z_new)
        vis = jnp.full(z_new.shape, new_key_visible)
        carry[...] = jnp.where(vis, lk_new, 0.0)
        acc[...] = jnp.where(vis, jnp.exp(ls_new), 0.0) * vnew_ref[0]

    for p in range(pps):
        kbf[p * page:(p + 1) * page, :] = k_refs[p][0].astype(BF16)
        vbf[p * page:(p + 1) * page, :] = v_refs[p][0].astype(BF16)

    z = _dot_nt(qbd, kbf[...]) + bias
    ls, lk = _log_sigmoid_pair(z)
    suffix = _suffix_matrix(MXU_DIM)
    run = carry[...]
    after = []
    for c in reversed(range(pps * page // MXU_DIM)):
        lkc = lk[:, c * MXU_DIM:(c + 1) * MXU_DIM]
        after.append(_dot(_split_bf16(lkc), suffix) + run)
        run = run + jnp.sum(lkc, axis=-1, keepdims=True)
    carry[...] = run
    w = jnp.exp(ls + jnp.concatenate(after[::-1], axis=1))
    acc[...] += _dot(w.astype(BF16), vbf[...])

    @pl.when(step == pl.num_programs(1) - 1)
    def _():
        a = acc[...]
        head = lax.broadcasted_iota(jnp.int32, a.shape, 0)
        lane_head = lax.broadcasted_iota(jnp.int32, a.shape, 1) // SB_HEAD_DIM
        o_ref[0] = jnp.sum(jnp.where(head == lane_head, a, 0.0), axis=0, keepdims=True)


def _sb_sample(page_table, qbd, k_new, v_new, bias_col, cache_k, cache_v, *, new_key_visible):
    nb, n_pages = page_table.shape
    _, page, width = cache_k.shape
    pps = PAGES_PER_STEP
    nstep = n_pages // pps
    rows = qbd.shape[1]

    def page_spec(i):
        return pl.BlockSpec((1, page, width), lambda b, c, pt: (pt[b, (nstep - 1 - c) * pps + i], 0, 0))

    per_batch = lambda r: pl.BlockSpec((1, r, width), lambda b, c, pt: (b, 0, 0))
    grid_spec = pltpu.PrefetchScalarGridSpec(
        num_scalar_prefetch=1,
        grid=(nb, nstep),
        in_specs=[per_batch(rows), per_batch(1), per_batch(1),
                  pl.BlockSpec((rows, 1), lambda b, c, pt: (0, 0))]
                 + [page_spec(i) for i in range(pps)] * 2,
        out_specs=per_batch(1),
        scratch_shapes=[pltpu.VMEM((pps * page, width), BF16), pltpu.VMEM((pps * page, width), BF16),
                        pltpu.VMEM((rows, width), F32), pltpu.VMEM((rows, 1), F32)],
    )
    return pl.pallas_call(
        functools.partial(_sb_sample_kernel, pps=pps, page=page, new_key_visible=new_key_visible),
        grid_spec=grid_spec,
        out_shape=jax.ShapeDtypeStruct((nb, 1, width), F32),
        compiler_params=_params("parallel", "arbitrary"),
        name="sb_sample",
    )(page_table, qbd, k_new, v_new, bias_col, *([cache_k] * pps), *([cache_v] * pps))


def _ssm_prep_kernel(are_ref, aim_ref, ldt_ref, bre_ref, bim_ref, bcat_ref, abr_ref, abi_ref):
    dt = jnp.exp(ldt_ref[...])
    lam_re = jnp.minimum(are_ref[...], A_RE_CLIP)
    lam_im = aim_ref[...]
    mag = jnp.exp(lam_re * dt)
    ab_re = mag * jnp.cos(lam_im * dt)
    ab_im = mag * jnp.sin(lam_im * dt)
    den = lam_re * lam_re + lam_im * lam_im
    nr = ab_re - 1.0
    f_re = (nr * lam_re + ab_im * lam_im) / den
    f_im = (ab_im * lam_re - nr * lam_im) / den
    abr_ref[...] = ab_re
    abi_ref[...] = ab_im
    br, bi = bre_ref[...], bim_ref[...]
    gp = br.shape[1]
    bcat_ref[:, :gp] = (f_re * br - f_im * bi).astype(BF16)
    bcat_ref[:, gp:] = (f_re * bi + f_im * br).astype(BF16)


def _ssm_prep(a_re, a_im, log_dt_full, b_re_bd, b_im_bd):
    gc, gp = b_re_bd.shape
    vec = _const_spec((1, gp))
    return pl.pallas_call(
        _ssm_prep_kernel,
        grid=(1,),
        in_specs=[vec, vec, vec, _const_spec((gc, gp)), _const_spec((gc, gp))],
        out_specs=[pl.BlockSpec((gc, 2 * gp), lambda i: (0, 0)),
                   pl.BlockSpec((1, gp), lambda i: (0, 0)), pl.BlockSpec((1, gp), lambda i: (0, 0))],
        out_shape=[jax.ShapeDtypeStruct((gc, 2 * gp), BF16),
                   jax.ShapeDtypeStruct((1, gp), F32), jax.ShapeDtypeStruct((1, gp), F32)],
        compiler_params=_params("arbitrary"),
        name="ssm_prep",
    )(a_re, a_im, log_dt_full, b_re_bd, b_im_bd)


def _gelu_tanh(y):
    return 0.5 * y * (1.0 + jnp.tanh(math.sqrt(2.0 / math.pi) * (y + 0.044715 * (y * y * y))))


def _ssm_kernel(*refs, steps, nb, has_init):
    (u_ref, bcat_ref, cre_ref, cim_ref, abr_ref, abi_ref, d_ref, wglu_ref, bglu_ref), refs = refs[:9], refs[9:]
    if has_init:
        (s0re_ref, s0im_ref), refs = refs[:2], refs[2:]
    o_ref, sfre_ref, sfim_ref, sre, sim, st_re, st_im = refs
    c = pl.program_id(0)
    gp = sre.shape[1]

    @pl.when(c == 0)
    def _():
        if has_init:
            st_re[...] = s0re_ref[...]
            st_im[...] = s0im_ref[...]
        else:
            st_re[...] = jnp.zeros_like(st_re)
            st_im[...] = jnp.zeros_like(st_im)

    u = u_ref[...]
    bu = _dot(u.astype(BF16), bcat_ref[...])
    sre[...] = bu[:, :gp]
    sim[...] = bu[:, gp:]
    a_re = jnp.broadcast_to(abr_ref[...], (nb, gp))
    a_im = jnp.broadcast_to(abi_ref[...], (nb, gp))

    def step(t, state):
        s_re, s_im = state
        rows = pl.ds(pl.multiple_of(t * nb, nb), nb)
        n_re = a_re * s_re - a_im * s_im + sre[rows, :]
        n_im = a_re * s_im + a_im * s_re + sim[rows, :]
        sre[rows, :] = n_re
        sim[rows, :] = n_im
        return n_re, n_im

    s_re, s_im = lax.fori_loop(0, steps, step, (st_re[...], st_im[...]))
    st_re[...] = s_re
    st_im[...] = s_im

    y = _dot(sre[...].astype(BF16), cre_ref[...]) - _dot(sim[...].astype(BF16), cim_ref[...])
    y = y + d_ref[...] * u
    g = _gelu_tanh(y)
    o_ref[...] = g * jax.nn.sigmoid(_dot(g.astype(BF16), wglu_ref[...]) + bglu_ref[...])

    @pl.when(c == pl.num_programs(0) - 1)
    def _():
        sfre_ref[...] = s_re
        sfim_ref[...] = s_im


def _ssm(u_tb, bcat, cre_bd, cim_bd, ab_re, ab_im, d_skip, w_glu, b_glu, s0, *, steps, nb):
    rows, width = u_tb.shape
    gp = ab_re.shape[1]
    chunk = steps * nb
    has_init = s0 is not None
    tile = pl.BlockSpec((chunk, width), lambda c: (c, 0))
    state = _const_spec((nb, gp))
    in_specs = [tile, _const_spec(bcat.shape), _const_spec(cre_bd.shape), _const_spec(cim_bd.shape),
                _const_spec((1, gp)), _const_spec((1, gp)), _const_spec((1, width)),
                _const_spec(w_glu.shape), _const_spec((1, width))]
    args = [u_tb, bcat, cre_bd, cim_bd, ab_re, ab_im, d_skip, w_glu, b_glu]
    if has_init:
        in_specs += [state, state]
        args += list(s0)
    return pl.pallas_call(
        functools.partial(_ssm_kernel, steps=steps, nb=nb, has_init=has_init),
        grid=(rows // chunk,),
        in_specs=in_specs,
        out_specs=[tile, pl.BlockSpec((nb, gp), lambda c: (0, 0)), pl.BlockSpec((nb, gp), lambda c: (0, 0))],
        out_shape=[jax.ShapeDtypeStruct((rows, width), F32),
                   jax.ShapeDtypeStruct((nb, gp), F32), jax.ShapeDtypeStruct((nb, gp), F32)],
        scratch_shapes=[pltpu.VMEM((chunk, gp), F32), pltpu.VMEM((chunk, gp), F32),
                        pltpu.VMEM((nb, gp), F32), pltpu.VMEM((nb, gp), F32)],
        compiler_params=_params("arbitrary"),
        name="ssm",
    )(*args)


def _mem_kv_kernel(m_ref, g_ref, wk_ref, wv_ref, k_ref, v_ref, kb_ref, vb_ref):
    h = _rms(m_ref[...], g_ref[...]).astype(BF16)
    k = _dot(h, wk_ref[...])
    v = _dot(h, wv_ref[...])
    k_ref[...] = k
    v_ref[...] = v
    kb_ref[...] = k.astype(BF16)
    vb_ref[...] = v.astype(BF16)


def _mem_kv(mem, g, wk, wv, *, tm):
    n, d = mem.shape
    row = pl.BlockSpec((tm, d), lambda i: (i, 0))
    return pl.pallas_call(
        _mem_kv_kernel,
        grid=(n // tm,),
        in_specs=[row, _const_spec((1, d)), _const_spec(wk.shape), _const_spec(wv.shape)],
        out_specs=[row] * 4,
        out_shape=[jax.ShapeDtypeStruct((n, d), F32)] * 2 + [jax.ShapeDtypeStruct((n, d), BF16)] * 2,
        compiler_params=_params("parallel"),
        name="mem_kv",
    )(mem, g, wk, wv)


def _mix_out_kernel(x1_ref, osb_ref, ossm_ref, gsb_ref, gssm_ref, wout_ref, gx_ref, wxq_ref, x2_ref, qx_ref, *, xscale):
    mixed = jnp.concatenate([_rms(osb_ref[...], gsb_ref[...]), _rms(ossm_ref[...], gssm_ref[...])], axis=-1)
    x2 = x1_ref[...] + _dot(mixed.astype(BF16), wout_ref[...])
    x2_ref[...] = x2
    hx = _rms(x2, gx_ref[...]).astype(BF16)
    qx_ref[...] = (_dot(hx, wxq_ref[...]) * xscale).astype(BF16)


def _mix_out(x1, o_sb, o_ssm, g_sb, g_ssm, w_out, g_x, w_xq, *, tm, ssm_index, xscale):
    n, d = x1.shape
    sbw = o_sb.shape[1]
    ssw = g_ssm.shape[1]
    row = lambda w: pl.BlockSpec((tm, w), lambda i: (i, 0))
    return pl.pallas_call(
        functools.partial(_mix_out_kernel, xscale=xscale),
        grid=(n // tm,),
        in_specs=[row(d), row(sbw), pl.BlockSpec((tm, ssw), ssm_index), _const_spec((1, sbw)), _const_spec((1, ssw)),
                  _const_spec(w_out.shape), _const_spec((1, d)), _const_spec(w_xq.shape)],
        out_specs=[row(d), row(d)],
        out_shape=[jax.ShapeDtypeStruct((n, d), F32), jax.ShapeDtypeStruct((n, d), BF16)],
        compiler_params=_params("parallel"),
        name="mix_out",
    )(x1, o_sb, o_ssm, g_sb, g_ssm, w_out, g_x, w_xq)


def _xattn_kernel(q_ref, k_ref, v_ref, o_ref, *, rows):
    q = q_ref[0]
    if q.shape[0] != rows:
        q = jnp.broadcast_to(q, (rows, q.shape[1]))
    hd = q.shape[1] // X_HEADS
    outs = []
    for h in range(X_HEADS):
        sl = slice(h * hd, (h + 1) * hd)
        s = _dot_nt(q[:, sl], k_ref[0, :, sl].astype(BF16))
        e = jnp.exp(s - jnp.max(s, axis=-1, keepdims=True))
        p = e / jnp.sum(e, axis=-1, keepdims=True)
        outs.append(_dot(p.astype(BF16), v_ref[0, :, sl].astype(BF16)))
    o = jnp.concatenate(outs, axis=-1).astype(o_ref.dtype)
    o_ref[0] = o[:o_ref.shape[1]]


def _xattn(q, mem_k, mem_v, *, tq):
    b, l, d = q.shape
    m = mem_k.shape[1]
    rows = max(tq, 16)
    qspec = pl.BlockSpec((1, tq, d), lambda bi, i: (bi, i, 0))
    mspec = pl.BlockSpec((1, m, d), lambda bi, i: (bi, 0, 0))
    return pl.pallas_call(
        functools.partial(_xattn_kernel, rows=rows),
        grid=(b, l // tq),
        in_specs=[qspec, mspec, mspec],
        out_specs=qspec,
        out_shape=jax.ShapeDtypeStruct((b, l, d), BF16),
        compiler_params=_params("parallel", "arbitrary"),
        name="xattn",
    )(q, mem_k, mem_v)


def _out_kernel(x2_ref, ox_ref, wxo_ref, g2_ref, wg_ref, wu_ref, wd_ref, gfin_ref, y_ref):
    x3 = x2_ref[...] + _dot(ox_ref[...], wxo_ref[...])
    x4 = _ffn_half(x3, g2_ref, wg_ref, wu_ref, wd_ref)
    y_ref[...] = _rms(x4, gfin_ref[...])


def _out(x2, ox, w_xo, g2, wg, wu, wd, g_fin, *, tm):
    n, d = x2.shape
    dff = wg.shape[1]
    row = pl.BlockSpec((tm, d), lambda i: (i, 0))
    return pl.pallas_call(
        _out_kernel,
        grid=(n // tm,),
        in_specs=[row, row, _const_spec(w_xo.shape), _const_spec((1, d)), _const_spec((d, dff)),
                  _const_spec((d, dff)), _const_spec((dff, d)), _const_spec((1, d))],
        out_specs=row,
        out_shape=jax.ShapeDtypeStruct((n, d), F32),
        compiler_params=_params("parallel"),
        name="ffn2_out",
    )(x2, ox, w_xo, g2, wg, wu, wd, g_fin)


def _block_diag(x):
    g, r, c = x.shape
    eye = jnp.eye(g, dtype=x.dtype)
    return (x[:, :, None, :] * eye[:, None, :, None]).reshape(g * r, g * c)


def kernel(x_prompt, x_sample, cache_k, cache_v, state_ssm_re, state_ssm_im, cache_mem_k, cache_mem_v, page_table, mem_prompt, g_ffn1, w_ffn1_gate, w_ffn1_up, w_ffn1_down, g_mix, w_in, sb_bias, ssm_a_re, ssm_a_im, ssm_log_dt, ssm_b_re, ssm_b_im, ssm_c_re, ssm_c_im, ssm_d, w_glu, b_glu, g_sb_out, g_ssm_out, w_out, g_xattn, g_mem, w_xq, w_xk, w_xv, w_xo, g_ffn2, w_ffn2_gate, w_ffn2_up, w_ffn2_down, g_final):
    depth = w_in.shape[0]
    assert depth == 1, "single-layer step"
    bsz, seq, d = x_prompt.shape
    dec_b, dec_seq, _ = x_sample.shape
    assert dec_seq == 1
    n_pool, page, sb_heads, hd = cache_k.shape[1:]
    assert hd == SB_HEAD_DIM
    sbw = sb_heads * hd
    groups, states = ssm_a_re.shape[1:]
    gp = groups * states
    ssw = groups * SSM_GROUP
    n_mem = mem_prompt.shape[1]
    past_len = page_table.shape[1] * page
    xscale = (d // X_HEADS) ** -0.5

    bf = lambda w: w[0].astype(BF16)
    vec = lambda g: g.reshape(1, -1)
    ffn1 = (vec(g_ffn1[0]), bf(w_ffn1_gate), bf(w_ffn1_up), bf(w_ffn1_down))
    ffn2 = (vec(g_ffn2[0]), bf(w_ffn2_gate), bf(w_ffn2_up), bf(w_ffn2_down))
    w_in_b, w_out_b, w_xq_b, w_xo_b = bf(w_in), bf(w_out), bf(w_xq), bf(w_xo)

    bcat, ab_re, ab_im = _ssm_prep(
        ssm_a_re[0].reshape(1, gp), ssm_a_im[0].reshape(1, gp),
        jnp.repeat(ssm_log_dt[0], states).reshape(1, gp),
        _block_diag(ssm_b_re[0].transpose(0, 2, 1)), _block_diag(ssm_b_im[0].transpose(0, 2, 1)))
    cre_bd = _block_diag(ssm_c_re[0].transpose(0, 2, 1)).astype(BF16)
    cim_bd = _block_diag(ssm_c_im[0].transpose(0, 2, 1)).astype(BF16)
    ssm_w = (bcat, cre_bd, cim_bd, ab_re, ab_im, vec(ssm_d[0]), bf(w_glu), vec(b_glu[0]))

    tm = PROMPT_TOKEN_TILE
    tiles = seq // tm
    tb_index = lambda i: (i % tiles, i // tiles)
    n = bsz * seq
    x1, k_p, v_p, qb, kb, vb, u_tb = _in_proj(
        x_prompt.reshape(n, d), *ffn1, vec(g_mix[0]), w_in_b,
        tm=tm, sb_width=sbw, u_shape=(seq, bsz * ssw), u_index=tb_index)
    o_sb = _sb_prompt(sb_bias[0], qb.reshape(bsz, seq, sbw), kb.reshape(bsz, seq, sbw), vb.reshape(bsz, seq, sbw),
                      blk=SB_BLOCK)
    o_ssm, sre_p, sim_p = _ssm(u_tb.reshape(seq * bsz, ssw), *ssm_w, None, steps=SSM_CHUNK_STEPS, nb=bsz)
    mk_p, mv_p, mkb, mvb = _mem_kv(mem_prompt.reshape(bsz * n_mem, d), vec(g_mem[0]), bf(w_xk), bf(w_xv), tm=tm)
    x2, qx = _mix_out(x1, o_sb.reshape(n, sbw), o_ssm.reshape(seq, bsz * ssw), vec(g_sb_out[0]), vec(g_ssm_out[0]),
                      w_out_b, vec(g_xattn[0]), w_xq_b, tm=tm, ssm_index=tb_index, xscale=xscale)
    ox = _xattn(qx.reshape(bsz, seq, d), mkb.reshape(bsz, n_mem, d), mvb.reshape(bsz, n_mem, d), tq=XATTN_Q_TILE)
    y_p = _out(x2, ox.reshape(n, d), w_xo_b, *ffn2, vec(g_final), tm=tm)

    row0 = lambda i: (0, 0)
    x1s, k_s, v_s, qbs, _, _, u_s = _in_proj(
        x_sample.reshape(dec_b, d), *ffn1, vec(g_mix[0]), w_in_b,
        tm=dec_b, sb_width=sbw, u_shape=(dec_b, ssw), u_index=row0)
    rows = 16
    head_of_lane = jnp.arange(sbw) // hd
    qbd = jnp.where(jnp.arange(rows)[None, :, None] == head_of_lane[None, None, :], qbs[:, None, :], 0).astype(BF16)
    bias_col = jnp.zeros((rows, 1), F32).at[:sb_heads, 0].set(sb_bias[0])
    o_sb_s = _sb_sample(page_table, qbd, k_s.reshape(dec_b, 1, sbw), v_s.reshape(dec_b, 1, sbw), bias_col,
                        cache_k[0].reshape(n_pool, page, sbw), cache_v[0].reshape(n_pool, page, sbw),
                        new_key_visible=bool(past_len < past_len))
    o_ssm_s, sre_s, sim_s = _ssm(u_s, *ssm_w, (state_ssm_re[0].reshape(dec_b, gp), state_ssm_im[0].reshape(dec_b, gp)),
                                 steps=1, nb=dec_b)
    x2s, qxs = _mix_out(x1s, o_sb_s.reshape(dec_b, sbw), o_ssm_s, vec(g_sb_out[0]), vec(g_ssm_out[0]),
                        w_out_b, vec(g_xattn[0]), w_xq_b, tm=dec_b, ssm_index=row0, xscale=xscale)
    oxs = _xattn(qxs.reshape(dec_b, 1, d), cache_mem_k[0].reshape(dec_b, n_mem, d),
                 cache_mem_v[0].reshape(dec_b, n_mem, d), tq=1)
    y_s = _out(x2s, oxs.reshape(dec_b, d), w_xo_b, *ffn2, vec(g_final), tm=dec_b)

    kv = lambda a, b_, l_: a.reshape(1, b_, l_, sb_heads, hd)
    st = lambda a, b_: a.reshape(1, b_, groups, states)
    mem = lambda a: a.reshape(1, bsz, n_mem, X_HEADS, d // X_HEADS)
    return (y_p.reshape(bsz, seq, d), y_s.reshape(dec_b, 1, d),
            kv(k_p, bsz, seq), kv(v_p, bsz, seq), kv(k_s, dec_b, 1), kv(v_s, dec_b, 1),
            st(sre_p, bsz), st(sim_p, bsz), st(sre_s, dec_b), st(sim_s, dec_b),
            mem(mk_p), mem(mv_p))
```

```python
import functools
import math

import jax
import jax.numpy as jnp
from jax import lax
from jax.experimental import pallas as pl
from jax.experimental.pallas import tpu as pltpu

F32 = jnp.float32
BF16 = jnp.bfloat16

EPS = 1e-6
A_RE_CLIP = -1e-4
SB_HEAD_DIM = 64
SSM_GROUP = 16
X_HEADS = 4

LANES = 128
MXU_DIM = 256
VMEM_LIMIT_BYTES = 56 * 1024 * 1024

PROMPT_TOKEN_TILE = 256
SB_BLOCK = 256
SSM_CHUNK_STEPS = 64
PAGES_PER_STEP = 8
XATTN_Q_TILE = 512


def _params(*semantics):
    return pltpu.CompilerParams(dimension_semantics=semantics, vmem_limit_bytes=VMEM_LIMIT_BYTES)


def _const_spec(shape):
    return pl.BlockSpec(shape, lambda *_: (0,) * len(shape), pipeline_mode=pl.Buffered(1))


def _dot(a, b):
    return jnp.dot(a, b, preferred_element_type=F32)


def _dot_nt(a, b):
    return lax.dot_general(a, b, (((1,), (1,)), ((), ())), preferred_element_type=F32)


def _rms(x, g):
    return x * lax.rsqrt(jnp.mean(x * x, axis=-1, keepdims=True) + EPS) * g


def _ffn_half(x, g_ref, wg_ref, wu_ref, wd_ref):
    h = _rms(x, g_ref[...]).astype(BF16)
    gate = _dot(h, wg_ref[...])
    up = _dot(h, wu_ref[...])
    act = (gate * jax.nn.sigmoid(gate) * up).astype(BF16)
    return x + 0.5 * _dot(act, wd_ref[...])


def _log_sigmoid_pair(z):
    ls = jnp.minimum(z, 0.0) - jnp.log(1.0 + jnp.exp(-jnp.abs(z)))
    return ls, ls - z


def _suffix_matrix(n, dtype):
    j = lax.broadcasted_iota(jnp.int32, (n, n), 0)
    s = lax.broadcasted_iota(jnp.int32, (n, n), 1)
    return jnp.where(j > s, 1.0, 0.0).astype(dtype)


def _suffix_sum(x, suffix):
    hi = x.astype(BF16)
    lo = (x - hi.astype(F32)).astype(BF16)
    return _dot(hi.astype(suffix.dtype), suffix) + _dot(lo.astype(suffix.dtype), suffix)


def _in_proj_kernel(x_ref, g1_ref, wg_ref, wu_ref, wd_ref, gmix_ref, win_ref,
                    x1_ref, k_ref, v_ref, qb_ref, kb_ref, vb_ref, u_ref, *, sb_width, kv_transposed):
    x1 = _ffn_half(x_ref[...], g1_ref, wg_ref, wu_ref, wd_ref)
    x1_ref[...] = x1
    h = _rms(x1, gmix_ref[...]).astype(BF16)
    proj = _dot(h, win_ref[...])
    q = proj[:, :sb_width]
    k = proj[:, sb_width:2 * sb_width]
    v = proj[:, 2 * sb_width:3 * sb_width]
    if kv_transposed:
        k_ref[0] = k.T
        v_ref[0] = v.T
    else:
        k_ref[...] = k
        v_ref[...] = v
    qb_ref[...] = (q * SB_HEAD_DIM ** -0.5).astype(BF16)
    kb_ref[...] = k.astype(BF16)
    vb_ref[...] = v.astype(BF16)
    u_ref[...] = proj[:, 3 * sb_width:]


def _in_proj(x, g1, wg, wu, wd, gmix, win, *, tm, sb_width, u_shape, u_index, kv_time_tiles=None):
    n, d = x.shape
    dff = wg.shape[1]
    ssm_width = win.shape[1] - 3 * sb_width
    row = lambda w: pl.BlockSpec((tm, w), lambda i: (i, 0))
    if kv_time_tiles is None:
        kv_spec = row(sb_width)
        kv_shape = jax.ShapeDtypeStruct((n, sb_width), F32)
    else:
        t = kv_time_tiles
        kv_spec = pl.BlockSpec((1, sb_width, tm), lambda i: (i // t, 0, i % t))
        kv_shape = jax.ShapeDtypeStruct((n // (t * tm), sb_width, t * tm), F32)
    return pl.pallas_call(
        functools.partial(_in_proj_kernel, sb_width=sb_width, kv_transposed=kv_time_tiles is not None),
        grid=(n // tm,),
        in_specs=[row(d), _const_spec((1, d)), _const_spec((d, dff)), _const_spec((d, dff)),
                  _const_spec((dff, d)), _const_spec((1, d)), _const_spec(win.shape)],
        out_specs=[row(d), kv_spec, kv_spec, row(sb_width), row(sb_width), row(sb_width),
                   pl.BlockSpec((tm, ssm_width), u_index)],
        out_shape=[jax.ShapeDtypeStruct((n, d), F32), kv_shape, kv_shape,
                   jax.ShapeDtypeStruct((n, sb_width), BF16), jax.ShapeDtypeStruct((n, sb_width), BF16),
                   jax.ShapeDtypeStruct((n, sb_width), BF16),
                   jax.ShapeDtypeStruct(u_shape, F32)],
        compiler_params=_params("parallel"),
        name="in_proj",
    )(x, g1, wg, wu, wd, gmix, win)


def _sb_prompt_kernel(bias_ref, q_ref, k_ref, v_ref, o_ref, carry_ref, acc_ref, z_ref, w_ref, *, blk):
    pair = pl.program_id(1)
    qi = pl.program_id(2)
    q = q_ref[0]
    lane = lax.broadcasted_iota(jnp.int32, q.shape, 1)
    first = lane < SB_HEAD_DIM
    zero = jnp.zeros_like(q)
    q2 = jnp.concatenate([jnp.where(first, q, zero), jnp.where(first, zero, q)], axis=0)
    bias2 = jnp.concatenate([jnp.full((blk, 1), bias_ref[2 * pair]), jnp.full((blk, 1), bias_ref[2 * pair + 1])], axis=0)
    suffix = _suffix_matrix(blk, BF16)

    def block_of(ref, j):
        return ref[0, pl.ds(pl.multiple_of(j * blk, blk), blk), :]

    def key_block(j, visible):
        z = z_ref[...] + bias2
        z_ref[...] = _dot_nt(q2, block_of(k_ref, jnp.maximum(j - 1, 0)))
        acc_ref[...] += _dot(w_ref[...], block_of(v_ref, jnp.minimum(j + 1, qi)))
        ls, lk = _log_sigmoid_pair(z)
        if visible is not None:
            lk = jnp.where(visible, lk, 0.0)
        carry = carry_ref[...]
        w = jnp.exp(ls + _suffix_sum(lk, suffix) + carry)
        if visible is not None:
            w = jnp.where(visible, w, 0.0)
        w_ref[...] = w.astype(BF16)
        carry_ref[...] = carry + jnp.sum(lk, axis=-1, keepdims=True)

    carry_ref[...] = jnp.zeros_like(carry_ref)
    acc_ref[...] = jnp.zeros_like(acc_ref)
    w_ref[...] = jnp.zeros_like(w_ref)
    z_ref[...] = _dot_nt(q2, block_of(k_ref, qi))
    qpos = lax.broadcasted_iota(jnp.int32, (2 * blk, blk), 0)
    kpos = lax.broadcasted_iota(jnp.int32, (2 * blk, blk), 1)
    key_block(qi, kpos < jnp.where(qpos >= blk, qpos - blk, qpos))

    @pl.loop(0, qi)
    def _(i):
        key_block(qi - 1 - i, None)

    acc = acc_ref[...] + _dot(w_ref[...], block_of(v_ref, 0))
    o_ref[0] = jnp.where(first, acc[:blk], acc[blk:])


def _sb_prompt(bias, qb, kb, vb, *, blk):
    b, l, w = qb.shape
    return pl.pallas_call(
        functools.partial(_sb_prompt_kernel, blk=blk),
        grid=(b, w // LANES, l // blk),
        in_specs=[pl.BlockSpec(memory_space=pltpu.SMEM),
                  pl.BlockSpec((1, blk, LANES), lambda bi, p, qi: (bi, qi, p)),
                  pl.BlockSpec((1, l, LANES), lambda bi, p, qi: (bi, 0, p)),
                  pl.BlockSpec((1, l, LANES), lambda bi, p, qi: (bi, 0, p))],
        out_specs=pl.BlockSpec((1, blk, LANES), lambda bi, p, qi: (bi, qi, p)),
        out_shape=jax.ShapeDtypeStruct((b, l, w), F32),
        scratch_shapes=[pltpu.VMEM((2 * blk, 1), F32), pltpu.VMEM((2 * blk, LANES), F32),
                        pltpu.VMEM((2 * blk, blk), F32), pltpu.VMEM((2 * blk, blk), BF16)],
        compiler_params=_params("parallel", "parallel", "arbitrary"),
        name="sb_prompt",
    )(bias, qb, kb, vb)


def _sb_sample_kernel(pt_ref, qrow_ref, qcol_ref, knew_ref, vnew_ref, bias_ref, *refs, pps, heads, new_key_visible):
    k_refs, v_refs = refs[:pps], refs[pps:2 * pps]
    o_ref, qb, acc, carry, wnew = refs[2 * pps:]
    step = pl.program_id(1)
    width, page = qb.shape
    hd = width // heads
    bias = bias_ref[...]

    def head_rows(x):
        head = lax.broadcasted_iota(jnp.int32, x.shape, 0)
        lane_head = lax.broadcasted_iota(jnp.int32, x.shape, 1) // hd
        return jnp.where(head == lane_head, x, 0.0)

    @pl.when(step == 0)
    def _():
        qb[...] = jnp.broadcast_to(qcol_ref[0], (width, page))
        acc[...] = jnp.zeros_like(acc)
        prod = jnp.broadcast_to(qrow_ref[0] * knew_ref[0], (heads, width))
        z_new = jnp.sum(head_rows(prod), axis=-1, keepdims=True) + bias
        ls_new, lk_new = _log_sigmoid_pair(z_new)
        vis = jnp.full(z_new.shape, new_key_visible)
        carry[...] = jnp.where(vis, lk_new, 0.0)
        wnew[...] = jnp.where(vis, jnp.exp(ls_new), 0.0)

    zs = []
    for p in range(pps):
        prod = k_refs[p][0] * qb[...]
        zs.append(jnp.concatenate(
            [jnp.sum(prod[h * hd:(h + 1) * hd, :], axis=0, keepdims=True) for h in range(heads)], axis=0))
    z = jnp.concatenate(zs, axis=1) + bias
    ls, lk = _log_sigmoid_pair(z)
    suffix = _suffix_matrix(MXU_DIM, F32)
    run = carry[...]
    after = []
    for c in reversed(range(pps * page // MXU_DIM)):
        lkc = lk[:, c * MXU_DIM:(c + 1) * MXU_DIM]
        after.append(_suffix_sum(lkc, suffix) + run)
        run = run + jnp.sum(lkc, axis=-1, keepdims=True)
    carry[...] = run
    w = jnp.exp(ls + jnp.concatenate(after[::-1], axis=1))

    for h in range(heads):
        rows = slice(h * hd, (h + 1) * hd)
        a = acc[rows, :]
        for p in range(pps):
            a = a + w[h:h + 1, p * page:(p + 1) * page] * v_refs[p][0, rows, :]
        acc[rows, :] = a

    @pl.when(step == pl.num_programs(1) - 1)
    def _():
        o_row = jnp.sum(acc[...].T, axis=0, keepdims=True)
        w_new_row = jnp.sum(head_rows(jnp.broadcast_to(wnew[...], (heads, width))), axis=0, keepdims=True)
        o_ref[0] = o_row + w_new_row * vnew_ref[0]


def _sb_sample(page_table, q, k_new, v_new, bias_col, cache_kt, cache_vt, *, new_key_visible):
    nb, n_pages = page_table.shape
    _, width, page = cache_kt.shape
    heads = bias_col.shape[0]
    pps = PAGES_PER_STEP
    nstep = n_pages // pps

    def page_spec(i):
        return pl.BlockSpec((1, width, page), lambda b, c, pt: (pt[b, (nstep - 1 - c) * pps + i], 0, 0))

    row = pl.BlockSpec((1, 1, width), lambda b, c, pt: (b, 0, 0))
    col = pl.BlockSpec((1, width, 1), lambda b, c, pt: (b, 0, 0))
    grid_spec = pltpu.PrefetchScalarGridSpec(
        num_scalar_prefetch=1,
        grid=(nb, nstep),
        in_specs=[row, col, row, row, pl.BlockSpec((heads, 1), lambda b, c, pt: (0, 0))]
                 + [page_spec(i) for i in range(pps)] * 2,
        out_specs=row,
        scratch_shapes=[pltpu.VMEM((width, page), F32), pltpu.VMEM((width, page), F32),
                        pltpu.VMEM((heads, 1), F32), pltpu.VMEM((heads, 1), F32)],
    )
    as_row = lambda a: a.reshape(nb, 1, width)
    return pl.pallas_call(
        functools.partial(_sb_sample_kernel, pps=pps, heads=heads, new_key_visible=new_key_visible),
        grid_spec=grid_spec,
        out_shape=jax.ShapeDtypeStruct((nb, 1, width), F32),
        compiler_params=_params("parallel", "arbitrary"),
        name="sb_sample",
    )(page_table, as_row(q), q.reshape(nb, width, 1), as_row(k_new), as_row(v_new), bias_col,
      *([cache_kt] * pps), *([cache_vt] * pps))


def _ssm_prep_kernel(are_ref, aim_ref, ldt_ref, bre_ref, bim_ref, bcat_ref, abr_ref, abi_ref):
    dt = jnp.exp(ldt_ref[...])
    lam_re = jnp.minimum(are_ref[...], A_RE_CLIP)
    lam_im = aim_ref[...]
    mag = jnp.exp(lam_re * dt)
    ab_re = mag * jnp.cos(lam_im * dt)
    ab_im = mag * jnp.sin(lam_im * dt)
    den = lam_re * lam_re + lam_im * lam_im
    nr = ab_re - 1.0
    f_re = (nr * lam_re + ab_im * lam_im) / den
    f_im = (ab_im * lam_re - nr * lam_im) / den
    abr_ref[...] = ab_re
    abi_ref[...] = ab_im
    br, bi = bre_ref[...], bim_ref[...]
    gp = br.shape[1]
    bcat_ref[:, :gp] = (f_re * br - f_im * bi).astype(BF16)
    bcat_ref[:, gp:] = (f_re * bi + f_im * br).astype(BF16)


def _ssm_prep(a_re, a_im, log_dt_full, b_re_bd, b_im_bd):
    gc, gp = b_re_bd.shape
    vec = _const_spec((1, gp))
    return pl.pallas_call(
        _ssm_prep_kernel,
        grid=(1,),
        in_specs=[vec, vec, vec, _const_spec((gc, gp)), _const_spec((gc, gp))],
        out_specs=[pl.BlockSpec((gc, 2 * gp), lambda i: (0, 0)),
                   pl.BlockSpec((1, gp), lambda i: (0, 0)), pl.BlockSpec((1, gp), lambda i: (0, 0))],
        out_shape=[jax.ShapeDtypeStruct((gc, 2 * gp), BF16),
                   jax.ShapeDtypeStruct((1, gp), F32), jax.ShapeDtypeStruct((1, gp), F32)],
        compiler_params=_params("arbitrary"),
        name="ssm_prep",
    )(a_re, a_im, log_dt_full, b_re_bd, b_im_bd)


def _gelu_tanh(y):
    return 0.5 * y * (1.0 + jnp.tanh(math.sqrt(2.0 / math.pi) * (y + 0.044715 * (y * y * y))))


def _ssm_kernel(*refs, steps, nb, has_init):
    (u_ref, bcat_ref, cre_ref, cim_ref, abr_ref, abi_ref, d_ref, wglu_ref, bglu_ref), refs = refs[:9], refs[9:]
    if has_init:
        (s0re_ref, s0im_ref), refs = refs[:2], refs[2:]
    o_ref, sfre_ref, sfim_ref, sre, sim, st_re, st_im = refs
    c = pl.program_id(0)
    gp = sre.shape[1]

    @pl.when(c == 0)
    def _():
        if has_init:
            st_re[...] = s0re_ref[...]
            st_im[...] = s0im_ref[...]
        else:
            st_re[...] = jnp.zeros_like(st_re)
            st_im[...] = jnp.zeros_like(st_im)

    u = u_ref[...]
    bu = _dot(u.astype(BF16), bcat_ref[...])
    sre[...] = bu[:, :gp]
    sim[...] = bu[:, gp:]
    a_re = jnp.broadcast_to(abr_ref[...], (nb, gp))
    a_im = jnp.broadcast_to(abi_ref[...], (nb, gp))

    def step(t, state):
        s_re, s_im = state
        rows = pl.ds(pl.multiple_of(t * nb, nb), nb)
        n_re = a_re * s_re - a_im * s_im + sre[rows, :]
        n_im = a_re * s_im + a_im * s_re + sim[rows, :]
        sre[rows, :] = n_re
        sim[rows, :] = n_im
        return n_re, n_im

    s_re, s_im = lax.fori_loop(0, steps, step, (st_re[...], st_im[...]))
    st_re[...] = s_re
    st_im[...] = s_im

    y = _dot(sre[...].astype(BF16), cre_ref[...]) - _dot(sim[...].astype(BF16), cim_ref[...])
    y = y + d_ref[...] * u
    g = _gelu_tanh(y)
    o_ref[...] = g * jax.nn.sigmoid(_dot(g.astype(BF16), wglu_ref[...]) + bglu_ref[...])

    @pl.when(c == pl.num_programs(0) - 1)
    def _():
        sfre_ref[...] = s_re
        sfim_ref[...] = s_im


def _ssm(u_tb, bcat, cre_bd, cim_bd, ab_re, ab_im, d_skip, w_glu, b_glu, s0, *, steps, nb):
    rows, width = u_tb.shape
    gp = ab_re.shape[1]
    chunk = steps * nb
    has_init = s0 is not None
    tile = pl.BlockSpec((chunk, width), lambda c: (c, 0))
    state = _const_spec((nb, gp))
    in_specs = [tile, _const_spec(bcat.shape), _const_spec(cre_bd.shape), _const_spec(cim_bd.shape),
                _const_spec((1, gp)), _const_spec((1, gp)), _const_spec((1, width)),
                _const_spec(w_glu.shape), _const_spec((1, width))]
    args = [u_tb, bcat, cre_bd, cim_bd, ab_re, ab_im, d_skip, w_glu, b_glu]
    if has_init:
        in_specs += [state, state]
        args += list(s0)
    return pl.pallas_call(
        functools.partial(_ssm_kernel, steps=steps, nb=nb, has_init=has_init),
        grid=(rows // chunk,),
        in_specs=in_specs,
        out_specs=[tile, pl.BlockSpec((nb, gp), lambda c: (0, 0)), pl.BlockSpec((nb, gp), lambda c: (0, 0))],
        out_shape=[jax.ShapeDtypeStruct((rows, width), F32),
                   jax.ShapeDtypeStruct((nb, gp), F32), jax.ShapeDtypeStruct((nb, gp), F32)],
        scratch_shapes=[pltpu.VMEM((chunk, gp), F32), pltpu.VMEM((chunk, gp), F32),
                        pltpu.VMEM((nb, gp), F32), pltpu.VMEM((nb, gp), F32)],
        compiler_params=_params("arbitrary"),
        name="ssm",
    )(*args)


def _mem_kv_kernel(m_ref, g_ref, wk_ref, wv_ref, k_ref, v_ref, kb_ref, vb_ref):
    h = _rms(m_ref[...], g_ref[...]).astype(BF16)
    k = _dot(h, wk_ref[...])
    v = _dot(h, wv_ref[...])
    k_ref[...] = k
    v_ref[...] = v
    kb_ref[...] = k.astype(BF16)
    vb_ref[...] = v.astype(BF16)


def _mem_kv(mem, g, wk, wv, *, tm):
    n, d = mem.shape
    row = pl.BlockSpec((tm, d), lambda i: (i, 0))
    return pl.pallas_call(
        _mem_kv_kernel,
        grid=(n // tm,),
        in_specs=[row, _const_spec((1, d)), _const_spec(wk.shape), _const_spec(wv.shape)],
        out_specs=[row] * 4,
        out_shape=[jax.ShapeDtypeStruct((n, d), F32)] * 2 + [jax.ShapeDtypeStruct((n, d), BF16)] * 2,
        compiler_params=_params("parallel"),
        name="mem_kv",
    )(mem, g, wk, wv)


def _mix_out_kernel(x1_ref, osb_ref, ossm_ref, gsb_ref, gssm_ref, wout_ref, gx_ref, wxq_ref, x2_ref, qx_ref, *, xscale):
    mixed = jnp.concatenate([_rms(osb_ref[...], gsb_ref[...]), _rms(ossm_ref[...], gssm_ref[...])], axis=-1)
    x2 = x1_ref[...] + _dot(mixed.astype(BF16), wout_ref[...])
    x2_ref[...] = x2
    hx = _rms(x2, gx_ref[...]).astype(BF16)
    qx_ref[...] = (_dot(hx, wxq_ref[...]) * xscale).astype(BF16)


def _mix_out(x1, o_sb, o_ssm, g_sb, g_ssm, w_out, g_x, w_xq, *, tm, ssm_index, xscale):
    n, d = x1.shape
    sbw = o_sb.shape[1]
    ssw = g_ssm.shape[1]
    row = lambda w: pl.BlockSpec((tm, w), lambda i: (i, 0))
    return pl.pallas_call(
        functools.partial(_mix_out_kernel, xscale=xscale),
        grid=(n // tm,),
        in_specs=[row(d), row(sbw), pl.BlockSpec((tm, ssw), ssm_index), _const_spec((1, sbw)), _const_spec((1, ssw)),
                  _const_spec(w_out.shape), _const_spec((1, d)), _const_spec(w_xq.shape)],
        out_specs=[row(d), row(d)],
        out_shape=[jax.ShapeDtypeStruct((n, d), F32), jax.ShapeDtypeStruct((n, d), BF16)],
        compiler_params=_params("parallel"),
        name="mix_out",
    )(x1, o_sb, o_ssm, g_sb, g_ssm, w_out, g_x, w_xq)


def _xattn_kernel(q_ref, k_ref, v_ref, o_ref, *, rows):
    q = q_ref[0]
    if q.shape[0] != rows:
        q = jnp.broadcast_to(q, (rows, q.shape[1]))
    hd = q.shape[1] // X_HEADS
    outs = []
    for h in range(X_HEADS):
        sl = slice(h * hd, (h + 1) * hd)
        s = _dot_nt(q[:, sl], k_ref[0, :, sl].astype(BF16))
        e = jnp.exp(s - jnp.max(s, axis=-1, keepdims=True))
        p = e / jnp.sum(e, axis=-1, keepdims=True)
        outs.append(_dot(p.astype(BF16), v_ref[0, :, sl].astype(BF16)))
    o = jnp.concatenate(outs, axis=-1).astype(o_ref.dtype)
    o_ref[0] = o[:o_ref.shape[1]]


def _xattn(q, mem_k, mem_v, *, tq):
    b, l, d = q.shape
    m = mem_k.shape[1]
    rows = max(tq, 16)
    qspec = pl.BlockSpec((1, tq, d), lambda bi, i: (bi, i, 0))
    mspec = pl.BlockSpec((1, m, d), lambda bi, i: (bi, 0, 0))
    return pl.pallas_call(
        functools.partial(_xattn_kernel, rows=rows),
        grid=(b, l // tq),
        in_specs=[qspec, mspec, mspec],
        out_specs=qspec,
        out_shape=jax.ShapeDtypeStruct((b, l, d), BF16),
        compiler_params=_params("parallel", "arbitrary"),
        name="xattn",
    )(q, mem_k, mem_v)


def _out_kernel(x2_ref, ox_ref, wxo_ref, g2_ref, wg_ref, wu_ref, wd_ref, gfin_ref, y_ref):
    x3 = x2_ref[...] + _dot(ox_ref[...], wxo_ref[...])
    x4 = _ffn_half(x3, g2_ref, wg_ref, wu_ref, wd_ref)
    y_ref[...] = _rms(x4, gfin_ref[...])


def _out(x2, ox, w_xo, g2, wg, wu, wd, g_fin, *, tm):
    n, d = x2.shape
    dff = wg.shape[1]
    row = pl.BlockSpec((tm, d), lambda i: (i, 0))
    return pl.pallas_call(
        _out_kernel,
        grid=(n // tm,),
        in_specs=[row, row, _const_spec(w_xo.shape), _const_spec((1, d)), _const_spec((d, dff)),
                  _const_spec((d, dff)), _const_spec((dff, d)), _const_spec((1, d))],
        out_specs=row,
        out_shape=jax.ShapeDtypeStruct((n, d), F32),
        compiler_params=_params("parallel"),
        name="ffn2_out",
    )(x2, ox, w_xo, g2, wg, wu, wd, g_fin)


def _block_diag(x):
    g, r, c = x.shape
    eye = jnp.eye(g, dtype=x.dtype)
    return (x[:, :, None, :] * eye[:, None, :, None]).reshape(g * r, g * c)


def kernel(x_prompt, x_sample, cache_k, cache_v, state_ssm_re, state_ssm_im, cache_mem_k, cache_mem_v, page_table, mem_prompt, g_ffn1, w_ffn1_gate, w_ffn1_up, w_ffn1_down, g_mix, w_in, sb_bias, ssm_a_re, ssm_a_im, ssm_log_dt, ssm_b_re, ssm_b_im, ssm_c_re, ssm_c_im, ssm_d, w_glu, b_glu, g_sb_out, g_ssm_out, w_out, g_xattn, g_mem, w_xq, w_xk, w_xv, w_xo, g_ffn2, w_ffn2_gate, w_ffn2_up, w_ffn2_down, g_final):
    depth = w_in.shape[0]
    assert depth == 1, "single-layer step"
    bsz, seq, d = x_prompt.shape
    dec_b, dec_seq, _ = x_sample.shape
    assert dec_seq == 1
    n_pool, page, sb_heads, hd = cache_k.shape[1:]
    assert hd == SB_HEAD_DIM
    sbw = sb_heads * hd
    groups, states = ssm_a_re.shape[1:]
    gp = groups * states
    ssw = groups * SSM_GROUP
    n_mem = mem_prompt.shape[1]
    past_len = page_table.shape[1] * page
    xscale = (d // X_HEADS) ** -0.5

    bf = lambda w: w[0].astype(BF16)
    vec = lambda g: g.reshape(1, -1)
    ffn1 = (vec(g_ffn1[0]), bf(w_ffn1_gate), bf(w_ffn1_up), bf(w_ffn1_down))
    ffn2 = (vec(g_ffn2[0]), bf(w_ffn2_gate), bf(w_ffn2_up), bf(w_ffn2_down))
    w_in_b, w_out_b, w_xq_b, w_xo_b = bf(w_in), bf(w_out), bf(w_xq), bf(w_xo)

    bcat, ab_re, ab_im = _ssm_prep(
        ssm_a_re[0].reshape(1, gp), ssm_a_im[0].reshape(1, gp),
        jnp.repeat(ssm_log_dt[0], states).reshape(1, gp),
        _block_diag(ssm_b_re[0].transpose(0, 2, 1)), _block_diag(ssm_b_im[0].transpose(0, 2, 1)))
    cre_bd = _block_diag(ssm_c_re[0].transpose(0, 2, 1)).astype(BF16)
    cim_bd = _block_diag(ssm_c_im[0].transpose(0, 2, 1)).astype(BF16)
    ssm_w = (bcat, cre_bd, cim_bd, ab_re, ab_im, vec(ssm_d[0]), bf(w_glu), vec(b_glu[0]))

    tm = PROMPT_TOKEN_TILE
    tiles = seq // tm
    tb_index = lambda i: (i % tiles, i // tiles)
    n = bsz * seq
    x1, k_p, v_p, qb, kb, vb, u_tb = _in_proj(
        x_prompt.reshape(n, d), *ffn1, vec(g_mix[0]), w_in_b,
        tm=tm, sb_width=sbw, u_shape=(seq, bsz * ssw), u_index=tb_index, kv_time_tiles=tiles)
    o_sb = _sb_prompt(sb_bias[0], qb.reshape(bsz, seq, sbw), kb.reshape(bsz, seq, sbw), vb.reshape(bsz, seq, sbw),
                      blk=SB_BLOCK)
    o_ssm, sre_p, sim_p = _ssm(u_tb.reshape(seq * bsz, ssw), *ssm_w, None, steps=SSM_CHUNK_STEPS, nb=bsz)
    mk_p, mv_p, mkb, mvb = _mem_kv(mem_prompt.reshape(bsz * n_mem, d), vec(g_mem[0]), bf(w_xk), bf(w_xv), tm=tm)
    x2, qx = _mix_out(x1, o_sb.reshape(n, sbw), o_ssm.reshape(seq, bsz * ssw), vec(g_sb_out[0]), vec(g_ssm_out[0]),
                      w_out_b, vec(g_xattn[0]), w_xq_b, tm=tm, ssm_index=tb_index, xscale=xscale)
    ox = _xattn(qx.reshape(bsz, seq, d), mkb.reshape(bsz, n_mem, d), mvb.reshape(bsz, n_mem, d), tq=XATTN_Q_TILE)
    y_p = _out(x2, ox.reshape(n, d), w_xo_b, *ffn2, vec(g_final), tm=tm)

    row0 = lambda i: (0, 0)
    x1s, k_s, v_s, qbs, _, _, u_s = _in_proj(
        x_sample.reshape(dec_b, d), *ffn1, vec(g_mix[0]), w_in_b,
        tm=dec_b, sb_width=sbw, u_shape=(dec_b, ssw), u_index=row0)
    pages_t = lambda c: c[0].transpose(0, 2, 3, 1).reshape(n_pool, sbw, page)
    o_sb_s = _sb_sample(page_table, qbs.astype(F32), k_s, v_s, sb_bias[0].reshape(sb_heads, 1),
                        pages_t(cache_k), pages_t(cache_v), new_key_visible=bool(past_len < past_len))
    o_ssm_s, sre_s, sim_s = _ssm(u_s, *ssm_w, (state_ssm_re[0].reshape(dec_b, gp), state_ssm_im[0].reshape(dec_b, gp)),
                                 steps=1, nb=dec_b)
    x2s, qxs = _mix_out(x1s, o_sb_s.reshape(dec_b, sbw), o_ssm_s, vec(g_sb_out[0]), vec(g_ssm_out[0]),
                        w_out_b, vec(g_xattn[0]), w_xq_b, tm=dec_b, ssm_index=row0, xscale=xscale)
    oxs = _xattn(qxs.reshape(dec_b, 1, d), cache_mem_k[0].reshape(dec_b, n_mem, d),
                 cache_mem_v[0].reshape(dec_b, n_mem, d), tq=1)
    y_s = _out(x2s, oxs.reshape(dec_b, d), w_xo_b, *ffn2, vec(g_final), tm=dec_b)

    kv = lambda a, b_, l_: a.reshape(1, b_, l_, sb_heads, hd)
    kv_t = lambda a: a.reshape(1, bsz, sb_heads, hd, seq).transpose(0, 1, 4, 2, 3)
    st = lambda a, b_: a.reshape(1, b_, groups, states)
    mem = lambda a: a.reshape(1, bsz, n_mem, X_HEADS, d // X_HEADS)
    return (y_p.reshape(bsz, seq, d), y_s.reshape(dec_b, 1, d),
            kv_t(k_p), kv_t(v_p), kv(k_s, dec_b, 1), kv(v_s, dec_b, 1),
            st(sre_p, bsz), st(sim_p, bsz), st(sre_s, dec_b), st(sim_s, dec_b),
            mem(mk_p), mem(mv_p))
```

```python
import functools
import math

import jax
import jax.numpy as jnp
from jax import lax
from jax.experimental import pallas as pl
from jax.experimental.pallas import tpu as pltpu

F32 = jnp.float32
BF16 = jnp.bfloat16

EPS = 1e-6
A_RE_CLIP = -1e-4
SB_HEAD_DIM = 64
SSM_GROUP = 16
X_HEADS = 4

LANES = 128
MXU_DIM = 256
VMEM_LIMIT_BYTES = 56 * 1024 * 1024

PROMPT_TOKEN_TILE = 256
SB_BLOCK = 256
SSM_CHUNK_STEPS = 64
PAGES_PER_STEP = 16
XATTN_Q_TILE = 512


def _params(*semantics):
    return pltpu.CompilerParams(dimension_semantics=semantics, vmem_limit_bytes=VMEM_LIMIT_BYTES)


def _const_spec(shape):
    return pl.BlockSpec(shape, lambda *_: (0,) * len(shape), pipeline_mode=pl.Buffered(1))


def _dot(a, b):
    return jnp.dot(a, b, preferred_element_type=F32)


def _dot_nt(a, b):
    return lax.dot_general(a, b, (((1,), (1,)), ((), ())), preferred_element_type=F32)


def _rms(x, g):
    return x * lax.rsqrt(jnp.mean(x * x, axis=-1, keepdims=True) + EPS) * g


def _ffn_half(x, g_ref, wg_ref, wu_ref, wd_ref):
    h = _rms(x, g_ref[...]).astype(BF16)
    gate = _dot(h, wg_ref[...])
    up = _dot(h, wu_ref[...])
    act = (gate * jax.nn.sigmoid(gate) * up).astype(BF16)
    return x + 0.5 * _dot(act, wd_ref[...])


def _log_sigmoid_pair(z):
    ls = jnp.minimum(z, 0.0) - jnp.log(1.0 + jnp.exp(-jnp.abs(z)))
    return ls, ls - z


def _suffix_matrix(n, dtype):
    j = lax.broadcasted_iota(jnp.int32, (n, n), 0)
    s = lax.broadcasted_iota(jnp.int32, (n, n), 1)
    return jnp.where(j > s, 1.0, 0.0).astype(dtype)


def _suffix_sum(x, suffix):
    hi = x.astype(BF16)
    lo = (x - hi.astype(F32)).astype(BF16)
    return _dot(hi.astype(suffix.dtype), suffix) + _dot(lo.astype(suffix.dtype), suffix)


def _in_proj_kernel(x_ref, g1_ref, wg_ref, wu_ref, wd_ref, gmix_ref, win_ref,
                    x1_ref, k_ref, v_ref, qb_ref, kb_ref, vb_ref, u_ref, *, sb_width, kv_transposed):
    x1 = _ffn_half(x_ref[...], g1_ref, wg_ref, wu_ref, wd_ref)
    x1_ref[...] = x1
    h = _rms(x1, gmix_ref[...]).astype(BF16)
    proj = _dot(h, win_ref[...])
    q = proj[:, :sb_width]
    k = proj[:, sb_width:2 * sb_width]
    v = proj[:, 2 * sb_width:3 * sb_width]
    if kv_transposed:
        k_ref[0] = k.T
        v_ref[0] = v.T
    else:
        k_ref[...] = k
        v_ref[...] = v
    qb_ref[...] = (q * SB_HEAD_DIM ** -0.5).astype(BF16)
    kb_ref[...] = k.astype(BF16)
    vb_ref[...] = v.astype(BF16)
    u_ref[...] = proj[:, 3 * sb_width:]


def _in_proj(x, g1, wg, wu, wd, gmix, win, *, tm, sb_width, u_shape, u_index, kv_time_tiles=None):
    n, d = x.shape
    dff = wg.shape[1]
    ssm_width = win.shape[1] - 3 * sb_width
    row = lambda w: pl.BlockSpec((tm, w), lambda i: (i, 0))
    if kv_time_tiles is None:
        kv_spec = row(sb_width)
        kv_shape = jax.ShapeDtypeStruct((n, sb_width), F32)
    else:
        t = kv_time_tiles
        kv_spec = pl.BlockSpec((1, sb_width, tm), lambda i: (i // t, 0, i % t))
        kv_shape = jax.ShapeDtypeStruct((n // (t * tm), sb_width, t * tm), F32)
    return pl.pallas_call(
        functools.partial(_in_proj_kernel, sb_width=sb_width, kv_transposed=kv_time_tiles is not None),
        grid=(n // tm,),
        in_specs=[row(d), _const_spec((1, d)), _const_spec((d, dff)), _const_spec((d, dff)),
                  _const_spec((dff, d)), _const_spec((1, d)), _const_spec(win.shape)],
        out_specs=[row(d), kv_spec, kv_spec, row(sb_width), row(sb_width), row(sb_width),
                   pl.BlockSpec((tm, ssm_width), u_index)],
        out_shape=[jax.ShapeDtypeStruct((n, d), F32), kv_shape, kv_shape,
                   jax.ShapeDtypeStruct((n, sb_width), BF16), jax.ShapeDtypeStruct((n, sb_width), BF16),
                   jax.ShapeDtypeStruct((n, sb_width), BF16),
                   jax.ShapeDtypeStruct(u_shape, F32)],
        compiler_params=_params("parallel"),
        name="in_proj",
    )(x, g1, wg, wu, wd, gmix, win)


def _sb_prompt_kernel(bias_ref, q_ref, k_ref, v_ref, o_ref, carry_ref, acc_ref, z_ref, w_ref, *, blk):
    pair = pl.program_id(1)
    nq = q_ref.shape[1] // blk
    lane = lax.broadcasted_iota(jnp.int32, (blk, LANES), 1)
    first = lane < SB_HEAD_DIM
    bias2 = jnp.concatenate(
        [jnp.full((blk, 1), bias_ref[2 * pair]), jnp.full((blk, 1), bias_ref[2 * pair + 1])], axis=0)
    suffix = _suffix_matrix(blk, BF16)

    def block_of(ref, j):
        return ref[0, pl.ds(pl.multiple_of(j * blk, blk), blk), :]

    def scores(qi, j):
        q = block_of(q_ref, qi)
        zero = jnp.zeros_like(q)
        q2 = jnp.concatenate([jnp.where(first, q, zero), jnp.where(first, zero, q)], axis=0)
        return _dot_nt(q2, block_of(k_ref, j))

    def finish_values(j_prev):
        return acc_ref[...] + _dot(w_ref[...], block_of(v_ref, j_prev))

    def write_out(qi, acc):
        o_ref[0, pl.ds(pl.multiple_of(qi * blk, blk), blk), :] = jnp.where(first, acc[:blk], acc[blk:])

    def weights(visible, carry):
        ls, lk = _log_sigmoid_pair(z_ref[...] + bias2)
        if visible is not None:
            lk = jnp.where(visible, lk, 0.0)
        w = jnp.exp(ls + _suffix_sum(lk, suffix) + carry)
        if visible is not None:
            w = jnp.where(visible, w, 0.0)
        return w.astype(BF16), jnp.sum(lk, axis=-1, keepdims=True)

    qpos = lax.broadcasted_iota(jnp.int32, (2 * blk, blk), 0)
    kpos = lax.broadcasted_iota(jnp.int32, (2 * blk, blk), 1)
    diagonal = kpos < jnp.where(qpos >= blk, qpos - blk, qpos)

    acc_ref[...] = jnp.zeros_like(acc_ref)
    w_ref[...] = jnp.zeros_like(w_ref)
    z_ref[...] = scores(0, 0)

    @pl.loop(0, nq)
    def _(qi):
        w, rowsum = weights(diagonal, 0.0)
        last = qi == nq - 1
        z_ref[...] = scores(jnp.where(qi == 0, jnp.where(last, 0, 1), qi),
                            jnp.where(qi == 0, jnp.where(last, 0, 1), qi - 1))
        write_out(jnp.maximum(qi - 1, 0), finish_values(0))
        acc_ref[...] = jnp.zeros_like(acc_ref)
        w_ref[...] = w
        carry_ref[...] = rowsum

        @pl.loop(0, qi)
        def _(i):
            j = qi - 1 - i
            carry = carry_ref[...]
            w, rowsum = weights(None, carry)
            end = j == 0
            nxt = jnp.minimum(qi + 1, nq - 1)
            z_ref[...] = scores(jnp.where(end, nxt, qi), jnp.where(end, nxt, j - 1))
            acc_ref[...] = finish_values(j + 1)
            w_ref[...] = w
            carry_ref[...] = carry + rowsum

    write_out(nq - 1, finish_values(0))


def _sb_prompt(bias, qb, kb, vb, *, blk):
    b, l, w = qb.shape
    seq_block = pl.BlockSpec((1, l, LANES), lambda bi, p: (bi, 0, p))
    return pl.pallas_call(
        functools.partial(_sb_prompt_kernel, blk=blk),
        grid=(b, w // LANES),
        in_specs=[pl.BlockSpec(memory_space=pltpu.SMEM), seq_block, seq_block, seq_block],
        out_specs=seq_block,
        out_shape=jax.ShapeDtypeStruct((b, l, w), F32),
        scratch_shapes=[pltpu.VMEM((2 * blk, 1), F32), pltpu.VMEM((2 * blk, LANES), F32),
                        pltpu.VMEM((2 * blk, blk), F32), pltpu.VMEM((2 * blk, blk), BF16)],
        compiler_params=_params("parallel", "parallel"),
        name="sb_prompt",
    )(bias, qb, kb, vb)


def _sb_sample_kernel(pt_ref, qrow_ref, knew_ref, vnew_ref, bias_ref, *refs, pps, heads, new_key_visible):
    k_refs, v_refs = refs[:pps], refs[pps:2 * pps]
    o_ref, qbd, acc, carry, wnew = refs[2 * pps:]
    step = pl.program_id(1)
    width, page = acc.shape
    hd = width // heads
    bias = bias_ref[...]

    def head_rows(x):
        head = lax.broadcasted_iota(jnp.int32, x.shape, 0)
        lane_head = lax.broadcasted_iota(jnp.int32, x.shape, 1) // hd
        return jnp.where(head == lane_head, x, 0.0)

    @pl.when(step == 0)
    def _():
        qbd[...] = head_rows(jnp.broadcast_to(qrow_ref[0], (heads, width)))
        acc[...] = jnp.zeros_like(acc)
        prod = jnp.broadcast_to(qrow_ref[0] * knew_ref[0], (heads, width))
        z_new = jnp.sum(head_rows(prod), axis=-1, keepdims=True) + bias
        ls_new, lk_new = _log_sigmoid_pair(z_new)
        vis = jnp.full(z_new.shape, new_key_visible)
        carry[...] = jnp.where(vis, lk_new, 0.0)
        wnew[...] = jnp.where(vis, jnp.exp(ls_new), 0.0)

    q_heads = qbd[...]
    z = jnp.concatenate([_dot(q_heads, k_refs[p][0]) for p in range(pps)], axis=1) + bias
    ls, lk = _log_sigmoid_pair(z)
    suffix = _suffix_matrix(MXU_DIM, F32)
    run = carry[...]
    after = []
    for c in reversed(range(pps * page // MXU_DIM)):
        lkc = lk[:, c * MXU_DIM:(c + 1) * MXU_DIM]
        after.append(_suffix_sum(lkc, suffix) + run)
        run = run + jnp.sum(lkc, axis=-1, keepdims=True)
    carry[...] = run
    w = jnp.exp(ls + jnp.concatenate(after[::-1], axis=1))

    for h in range(heads):
        rows = slice(h * hd, (h + 1) * hd)
        a = acc[rows, :]
        for p in range(pps):
            a = a + w[h:h + 1, p * page:(p + 1) * page] * v_refs[p][0, rows, :]
        acc[rows, :] = a

    @pl.when(step == pl.num_programs(1) - 1)
    def _():
        o_row = jnp.sum(acc[...].T, axis=0, keepdims=True)
        w_new_row = jnp.sum(head_rows(jnp.broadcast_to(wnew[...], (heads, width))), axis=0, keepdims=True)
        o_ref[0] = o_row + w_new_row * vnew_ref[0]


def _sb_sample(page_table, q, k_new, v_new, bias_col, cache_kt, cache_vt, *, new_key_visible):
    nb, n_pages = page_table.shape
    _, width, page = cache_kt.shape
    heads = bias_col.shape[0]
    pps = PAGES_PER_STEP
    nstep = n_pages // pps

    def page_spec(i):
        return pl.BlockSpec((1, width, page), lambda b, c, pt: (pt[b, (nstep - 1 - c) * pps + i], 0, 0))

    row = pl.BlockSpec((1, 1, width), lambda b, c, pt: (b, 0, 0))
    grid_spec = pltpu.PrefetchScalarGridSpec(
        num_scalar_prefetch=1,
        grid=(nb, nstep),
        in_specs=[row, row, row, pl.BlockSpec((heads, 1), lambda b, c, pt: (0, 0))]
                 + [page_spec(i) for i in range(pps)] * 2,
        out_specs=row,
        scratch_shapes=[pltpu.VMEM((heads, width), F32), pltpu.VMEM((width, page), F32),
                        pltpu.VMEM((heads, 1), F32), pltpu.VMEM((heads, 1), F32)],
    )
    as_row = lambda a: a.reshape(nb, 1, width)
    return pl.pallas_call(
        functools.partial(_sb_sample_kernel, pps=pps, heads=heads, new_key_visible=new_key_visible),
        grid_spec=grid_spec,
        out_shape=jax.ShapeDtypeStruct((nb, 1, width), F32),
        compiler_params=_params("parallel", "arbitrary"),
        name="sb_sample",
    )(page_table, as_row(q), as_row(k_new), as_row(v_new), bias_col,
      *([cache_kt] * pps), *([cache_vt] * pps))


def _ssm_prep_kernel(are_ref, aim_ref, ldt_ref, bre_ref, bim_ref, bcat_ref, abr_ref, abi_ref):
    dt = jnp.exp(ldt_ref[...])
    lam_re = jnp.minimum(are_ref[...], A_RE_CLIP)
    lam_im = aim_ref[...]
    mag = jnp.exp(lam_re * dt)
    ab_re = mag * jnp.cos(lam_im * dt)
    ab_im = mag * jnp.sin(lam_im * dt)
    den = lam_re * lam_re + lam_im * lam_im
    nr = ab_re - 1.0
    f_re = (nr * lam_re + ab_im * lam_im) / den
    f_im = (ab_im * lam_re - nr * lam_im) / den
    abr_ref[...] = ab_re
    abi_ref[...] = ab_im
    br, bi = bre_ref[...], bim_ref[...]
    gp = br.shape[1]
    bcat_ref[:, :gp] = (f_re * br - f_im * bi).astype(BF16)
    bcat_ref[:, gp:] = (f_re * bi + f_im * br).astype(BF16)


def _ssm_prep(a_re, a_im, log_dt_full, b_re_bd, b_im_bd):
    gc, gp = b_re_bd.shape
    vec = _const_spec((1, gp))
    return pl.pallas_call(
        _ssm_prep_kernel,
        grid=(1,),
        in_specs=[vec, vec, vec, _const_spec((gc, gp)), _const_spec((gc, gp))],
        out_specs=[pl.BlockSpec((gc, 2 * gp), lambda i: (0, 0)),
                   pl.BlockSpec((1, gp), lambda i: (0, 0)), pl.BlockSpec((1, gp), lambda i: (0, 0))],
        out_shape=[jax.ShapeDtypeStruct((gc, 2 * gp), BF16),
                   jax.ShapeDtypeStruct((1, gp), F32), jax.ShapeDtypeStruct((1, gp), F32)],
        compiler_params=_params("arbitrary"),
        name="ssm_prep",
    )(a_re, a_im, log_dt_full, b_re_bd, b_im_bd)


def _gelu_tanh(y):
    return 0.5 * y * (1.0 + jnp.tanh(math.sqrt(2.0 / math.pi) * (y + 0.044715 * (y * y * y))))


def _ssm_kernel(*refs, steps, nb, has_init):
    (u_ref, bcat_ref, cre_ref, cim_ref, abr_ref, abi_ref, d_ref, wglu_ref, bglu_ref), refs = refs[:9], refs[9:]
    if has_init:
        (s0re_ref, s0im_ref), refs = refs[:2], refs[2:]
    o_ref, sfre_ref, sfim_ref, sre, sim, st_re, st_im = refs
    c = pl.program_id(0)
    gp = sre.shape[1]

    @pl.when(c == 0)
    def _():
        if has_init:
            st_re[...] = s0re_ref[...]
            st_im[...] = s0im_ref[...]
        else:
            st_re[...] = jnp.zeros_like(st_re)
            st_im[...] = jnp.zeros_like(st_im)

    u = u_ref[...]
    bu = _dot(u.astype(BF16), bcat_ref[...])
    sre[...] = bu[:, :gp]
    sim[...] = bu[:, gp:]
    a_re = jnp.broadcast_to(abr_ref[...], (nb, gp))
    a_im = jnp.broadcast_to(abi_ref[...], (nb, gp))

    def step(t, state):
        s_re, s_im = state
        rows = pl.ds(pl.multiple_of(t * nb, nb), nb)
        n_re = a_re * s_re - a_im * s_im + sre[rows, :]
        n_im = a_re * s_im + a_im * s_re + sim[rows, :]
        sre[rows, :] = n_re
        sim[rows, :] = n_im
        return n_re, n_im

    s_re, s_im = lax.fori_loop(0, steps, step, (st_re[...], st_im[...]))
    st_re[...] = s_re
    st_im[...] = s_im

    y = _dot(sre[...].astype(BF16), cre_ref[...]) - _dot(sim[...].astype(BF16), cim_ref[...])
    y = y + d_ref[...] * u
    g = _gelu_tanh(y)
    o_ref[...] = g * jax.nn.sigmoid(_dot(g.astype(BF16), wglu_ref[...]) + bglu_ref[...])

    @pl.when(c == pl.num_programs(0) - 1)
    def _():
        sfre_ref[...] = s_re
        sfim_ref[...] = s_im


def _ssm(u_tb, bcat, cre_bd, cim_bd, ab_re, ab_im, d_skip, w_glu, b_glu, s0, *, steps, nb):
    rows, width = u_tb.shape
    gp = ab_re.shape[1]
    chunk = steps * nb
    has_init = s0 is not None
    tile = pl.BlockSpec((chunk, width), lambda c: (c, 0))
    state = _const_spec((nb, gp))
    in_specs = [tile, _const_spec(bcat.shape), _const_spec(cre_bd.shape), _const_spec(cim_bd.shape),
                _const_spec((1, gp)), _const_spec((1, gp)), _const_spec((1, width)),
                _const_spec(w_glu.shape), _const_spec((1, width))]
    args = [u_tb, bcat, cre_bd, cim_bd, ab_re, ab_im, d_skip, w_glu, b_glu]
    if has_init:
        in_specs += [state, state]
        args += list(s0)
    return pl.pallas_call(
        functools.partial(_ssm_kernel, steps=steps, nb=nb, has_init=has_init),
        grid=(rows // chunk,),
        in_specs=in_specs,
        out_specs=[tile, pl.BlockSpec((nb, gp), lambda c: (0, 0)), pl.BlockSpec((nb, gp), lambda c: (0, 0))],
        out_shape=[jax.ShapeDtypeStruct((rows, width), F32),
                   jax.ShapeDtypeStruct((nb, gp), F32), jax.ShapeDtypeStruct((nb, gp), F32)],
        scratch_shapes=[pltpu.VMEM((chunk, gp), F32), pltpu.VMEM((chunk, gp), F32),
                        pltpu.VMEM((nb, gp), F32), pltpu.VMEM((nb, gp), F32)],
        compiler_params=_params("arbitrary"),
        name="ssm",
    )(*args)


def _mem_kv_kernel(m_ref, g_ref, wk_ref, wv_ref, k_ref, v_ref, kb_ref, vb_ref):
    h = _rms(m_ref[...], g_ref[...]).astype(BF16)
    k = _dot(h, wk_ref[...])
    v = _dot(h, wv_ref[...])
    k_ref[...] = k
    v_ref[...] = v
    kb_ref[...] = k.astype(BF16)
    vb_ref[...] = v.astype(BF16)


def _mem_kv(mem, g, wk, wv, *, tm):
    n, d = mem.shape
    row = pl.BlockSpec((tm, d), lambda i: (i, 0))
    return pl.pallas_call(
        _mem_kv_kernel,
        grid=(n // tm,),
        in_specs=[row, _const_spec((1, d)), _const_spec(wk.shape), _const_spec(wv.shape)],
        out_specs=[row] * 4,
        out_shape=[jax.ShapeDtypeStruct((n, d), F32)] * 2 + [jax.ShapeDtypeStruct((n, d), BF16)] * 2,
        compiler_params=_params("parallel"),
        name="mem_kv",
    )(mem, g, wk, wv)


def _mix_out_kernel(x1_ref, osb_ref, ossm_ref, gsb_ref, gssm_ref, wout_ref, gx_ref, wxq_ref, x2_ref, qx_ref, *, xscale):
    mixed = jnp.concatenate([_rms(osb_ref[...], gsb_ref[...]), _rms(ossm_ref[...], gssm_ref[...])], axis=-1)
    x2 = x1_ref[...] + _dot(mixed.astype(BF16), wout_ref[...])
    x2_ref[...] = x2
    hx = _rms(x2, gx_ref[...]).astype(BF16)
    qx_ref[...] = (_dot(hx, wxq_ref[...]) * xscale).astype(BF16)


def _mix_out(x1, o_sb, o_ssm, g_sb, g_ssm, w_out, g_x, w_xq, *, tm, ssm_index, xscale):
    n, d = x1.shape
    sbw = o_sb.shape[1]
    ssw = g_ssm.shape[1]
    row = lambda w: pl.BlockSpec((tm, w), lambda i: (i, 0))
    return pl.pallas_call(
        functools.partial(_mix_out_kernel, xscale=xscale),
        grid=(n // tm,),
        in_specs=[row(d), row(sbw), pl.BlockSpec((tm, ssw), ssm_index), _const_spec((1, sbw)), _const_spec((1, ssw)),
                  _const_spec(w_out.shape), _const_spec((1, d)), _const_spec(w_xq.shape)],
        out_specs=[row(d), row(d)],
        out_shape=[jax.ShapeDtypeStruct((n, d), F32), jax.ShapeDtypeStruct((n, d), BF16)],
        compiler_params=_params("parallel"),
        name="mix_out",
    )(x1, o_sb, o_ssm, g_sb, g_ssm, w_out, g_x, w_xq)


def _xattn_kernel(q_ref, k_ref, v_ref, o_ref, *, rows):
    q = q_ref[0]
    if q.shape[0] != rows:
        q = jnp.broadcast_to(q, (rows, q.shape[1]))
    hd = q.shape[1] // X_HEADS

    def mem_head(ref, h):
        return ref[0, :, h * hd:(h + 1) * hd].astype(BF16)

    outs = []
    for h in range(X_HEADS):
        s = _dot_nt(q[:, h * hd:(h + 1) * hd], mem_head(k_ref, h))
        e = jnp.exp(s - jnp.max(s, axis=-1, keepdims=True))
        p = e / jnp.sum(e, axis=-1, keepdims=True)
        outs.append(_dot(p.astype(BF16), mem_head(v_ref, h)))
    o = jnp.concatenate(outs, axis=-1).astype(o_ref.dtype)
    o_ref[0] = o[:o_ref.shape[1]]


def _xattn(q, mem_k, mem_v, *, tq):
    b, l, d = q.shape
    m = mem_k.shape[1]
    rows = max(tq, 16)
    qspec = pl.BlockSpec((1, tq, d), lambda bi, i: (bi, i, 0))
    mspec = pl.BlockSpec((1, m, d), lambda bi, i: (bi, 0, 0))
    return pl.pallas_call(
        functools.partial(_xattn_kernel, rows=rows),
        grid=(b, l // tq),
        in_specs=[qspec, mspec, mspec],
        out_specs=qspec,
        out_shape=jax.ShapeDtypeStruct((b, l, d), BF16),
        compiler_params=_params("parallel", "arbitrary"),
        name="xattn",
    )(q, mem_k, mem_v)


def _out_kernel(x2_ref, ox_ref, wxo_ref, g2_ref, wg_ref, wu_ref, wd_ref, gfin_ref, y_ref):
    x3 = x2_ref[...] + _dot(ox_ref[...], wxo_ref[...])
    x4 = _ffn_half(x3, g2_ref, wg_ref, wu_ref, wd_ref)
    y_ref[...] = _rms(x4, gfin_ref[...])


def _out(x2, ox, w_xo, g2, wg, wu, wd, g_fin, *, tm):
    n, d = x2.shape
    dff = wg.shape[1]
    row = pl.BlockSpec((tm, d), lambda i: (i, 0))
    return pl.pallas_call(
        _out_kernel,
        grid=(n // tm,),
        in_specs=[row, row, _const_spec(w_xo.shape), _const_spec((1, d)), _const_spec((d, dff)),
                  _const_spec((d, dff)), _const_spec((dff, d)), _const_spec((1, d))],
        out_specs=row,
        out_shape=jax.ShapeDtypeStruct((n, d), F32),
        compiler_params=_params("parallel"),
        name="ffn2_out",
    )(x2, ox, w_xo, g2, wg, wu, wd, g_fin)


def _block_diag(x):
    g, r, c = x.shape
    eye = jnp.eye(g, dtype=x.dtype)
    return (x[:, :, None, :] * eye[:, None, :, None]).reshape(g * r, g * c)


def kernel(x_prompt, x_sample, cache_k, cache_v, state_ssm_re, state_ssm_im, cache_mem_k, cache_mem_v, page_table, mem_prompt, g_ffn1, w_ffn1_gate, w_ffn1_up, w_ffn1_down, g_mix, w_in, sb_bias, ssm_a_re, ssm_a_im, ssm_log_dt, ssm_b_re, ssm_b_im, ssm_c_re, ssm_c_im, ssm_d, w_glu, b_glu, g_sb_out, g_ssm_out, w_out, g_xattn, g_mem, w_xq, w_xk, w_xv, w_xo, g_ffn2, w_ffn2_gate, w_ffn2_up, w_ffn2_down, g_final):
    depth = w_in.shape[0]
    assert depth == 1, "single-layer step"
    bsz, seq, d = x_prompt.shape
    dec_b, dec_seq, _ = x_sample.shape
    assert dec_seq == 1
    n_pool, page, sb_heads, hd = cache_k.shape[1:]
    assert hd == SB_HEAD_DIM
    sbw = sb_heads * hd
    groups, states = ssm_a_re.shape[1:]
    gp = groups * states
    ssw = groups * SSM_GROUP
    n_mem = mem_prompt.shape[1]
    past_len = page_table.shape[1] * page
    xscale = (d // X_HEADS) ** -0.5

    bf = lambda w: w[0].astype(BF16)
    vec = lambda g: g.reshape(1, -1)
    ffn1 = (vec(g_ffn1[0]), bf(w_ffn1_gate), bf(w_ffn1_up), bf(w_ffn1_down))
    ffn2 = (vec(g_ffn2[0]), bf(w_ffn2_gate), bf(w_ffn2_up), bf(w_ffn2_down))
    w_in_b, w_out_b, w_xq_b, w_xo_b = bf(w_in), bf(w_out), bf(w_xq), bf(w_xo)

    bcat, ab_re, ab_im = _ssm_prep(
        ssm_a_re[0].reshape(1, gp), ssm_a_im[0].reshape(1, gp),
        jnp.repeat(ssm_log_dt[0], states).reshape(1, gp),
        _block_diag(ssm_b_re[0].transpose(0, 2, 1)), _block_diag(ssm_b_im[0].transpose(0, 2, 1)))
    cre_bd = _block_diag(ssm_c_re[0].transpose(0, 2, 1)).astype(BF16)
    cim_bd = _block_diag(ssm_c_im[0].transpose(0, 2, 1)).astype(BF16)
    ssm_w = (bcat, cre_bd, cim_bd, ab_re, ab_im, vec(ssm_d[0]), bf(w_glu), vec(b_glu[0]))

    tm = PROMPT_TOKEN_TILE
    tiles = seq // tm
    tb_index = lambda i: (i % tiles, i // tiles)
    n = bsz * seq
    x1, k_p, v_p, qb, kb, vb, u_tb = _in_proj(
        x_prompt.reshape(n, d), *ffn1, vec(g_mix[0]), w_in_b,
        tm=tm, sb_width=sbw, u_shape=(seq, bsz * ssw), u_index=tb_index, kv_time_tiles=tiles)
    o_sb = _sb_prompt(sb_bias[0], qb.reshape(bsz, seq, sbw), kb.reshape(bsz, seq, sbw), vb.reshape(bsz, seq, sbw),
                      blk=SB_BLOCK)
    o_ssm, sre_p, sim_p = _ssm(u_tb.reshape(seq * bsz, ssw), *ssm_w, None, steps=SSM_CHUNK_STEPS, nb=bsz)
    mk_p, mv_p, mkb, mvb = _mem_kv(mem_prompt.reshape(bsz * n_mem, d), vec(g_mem[0]), bf(w_xk), bf(w_xv), tm=tm)
    x2, qx = _mix_out(x1, o_sb.reshape(n, sbw), o_ssm.reshape(seq, bsz * ssw), vec(g_sb_out[0]), vec(g_ssm_out[0]),
                      w_out_b, vec(g_xattn[0]), w_xq_b, tm=tm, ssm_index=tb_index, xscale=xscale)
    ox = _xattn(qx.reshape(bsz, seq, d), mkb.reshape(bsz, n_mem, d), mvb.reshape(bsz, n_mem, d), tq=XATTN_Q_TILE)
    y_p = _out(x2, ox.reshape(n, d), w_xo_b, *ffn2, vec(g_final), tm=tm)

    row0 = lambda i: (0, 0)
    x1s, k_s, v_s, qbs, _, _, u_s = _in_proj(
        x_sample.reshape(dec_b, d), *ffn1, vec(g_mix[0]), w_in_b,
        tm=dec_b, sb_width=sbw, u_shape=(dec_b, ssw), u_index=row0)
    pages_t = lambda c: c[0].transpose(0, 2, 3, 1).reshape(n_pool, sbw, page)
    o_sb_s = _sb_sample(page_table, qbs.astype(F32), k_s, v_s, sb_bias[0].reshape(sb_heads, 1),
                        pages_t(cache_k), pages_t(cache_v), new_key_visible=bool(past_len < past_len))
    o_ssm_s, sre_s, sim_s = _ssm(u_s, *ssm_w, (state_ssm_re[0].reshape(dec_b, gp), state_ssm_im[0].reshape(dec_b, gp)),
                                 steps=1, nb=dec_b)
    x2s, qxs = _mix_out(x1s, o_sb_s.reshape(dec_b, sbw), o_ssm_s, vec(g_sb_out[0]), vec(g_ssm_out[0]),
                        w_out_b, vec(g_xattn[0]), w_xq_b, tm=dec_b, ssm_index=row0, xscale=xscale)
    mem_s = lambda c: c[0].reshape(dec_b, n_mem, d).astype(BF16)
    oxs = _xattn(qxs.reshape(dec_b, 1, d), mem_s(cache_mem_k), mem_s(cache_mem_v), tq=1)
    y_s = _out(x2s, oxs.reshape(dec_b, d), w_xo_b, *ffn2, vec(g_final), tm=dec_b)

    kv = lambda a, b_, l_: a.reshape(1, b_, l_, sb_heads, hd)
    kv_t = lambda a: a.reshape(1, bsz, sb_heads, hd, seq).transpose(0, 1, 4, 2, 3)
    st = lambda a, b_: a.reshape(1, b_, groups, states)
    mem = lambda a: a.reshape(1, bsz, n_mem, X_HEADS, d // X_HEADS)
    return (y_p.reshape(bsz, seq, d), y_s.reshape(dec_b, 1, d),
            kv_t(k_p), kv_t(v_p), kv(k_s, dec_b, 1), kv(v_s, dec_b, 1),
            st(sre_p, bsz), st(sim_p, bsz), st(sre_s, dec_b), st(sim_s, dec_b),
            mem(mk_p), mem(mv_p))
```

```python
import functools
import math

import jax
import jax.numpy as jnp
from jax import lax
from jax.experimental import pallas as pl
from jax.experimental.pallas import tpu as pltpu
from jax.experimental.pallas import tpu_sc as plsc

F32 = jnp.float32
BF16 = jnp.bfloat16

EPS = 1e-6
A_RE_CLIP = -1e-4
SB_HEAD_DIM = 64
SSM_GROUP = 16
X_HEADS = 4

LANES = 128
MXU_DIM = 256
VMEM_LIMIT_BYTES = 56 * 1024 * 1024
SC_CORES = 2
SC_SUBCORES = 16
SC_LANES = 16
SC_GATHER_SLABS = 4
SC_DIM_CHUNK = 8

PROMPT_TOKEN_TILE = 256
SB_BLOCK = 256
SSM_CHUNK_STEPS = 64
SB_SAMPLE_KEY_CHUNK = 1024
XATTN_Q_TILE = 512


def _params(*semantics):
    return pltpu.CompilerParams(dimension_semantics=semantics, vmem_limit_bytes=VMEM_LIMIT_BYTES)


def _const_spec(shape):
    return pl.BlockSpec(shape, lambda *_: (0,) * len(shape), pipeline_mode=pl.Buffered(1))


def _dot(a, b):
    return jnp.dot(a, b, preferred_element_type=F32)


def _dot_nt(a, b):
    return lax.dot_general(a, b, (((1,), (1,)), ((), ())), preferred_element_type=F32)


def _rms(x, g):
    return x * lax.rsqrt(jnp.mean(x * x, axis=-1, keepdims=True) + EPS) * g


def _ffn_half(x, g_ref, wg_ref, wu_ref, wd_ref):
    h = _rms(x, g_ref[...]).astype(BF16)
    gate = _dot(h, wg_ref[...])
    up = _dot(h, wu_ref[...])
    act = (gate * jax.nn.sigmoid(gate) * up).astype(BF16)
    return x + 0.5 * _dot(act, wd_ref[...])


def _log_sigmoid_pair(z):
    ls = jnp.minimum(z, 0.0) - jnp.log(1.0 + jnp.exp(-jnp.abs(z)))
    return ls, ls - z


def _suffix_matrix(n, dtype):
    j = lax.broadcasted_iota(jnp.int32, (n, n), 0)
    s = lax.broadcasted_iota(jnp.int32, (n, n), 1)
    return jnp.where(j > s, 1.0, 0.0).astype(dtype)


def _split_dot(x, m):
    hi = x.astype(BF16)
    lo = (x - hi.astype(F32)).astype(BF16)
    return _dot(hi, m) + _dot(lo, m)


def _in_proj_kernel(x_ref, g1_ref, wg_ref, wu_ref, wd_ref, gmix_ref, win_ref,
                    x1_ref, k_ref, v_ref, qb_ref, kb_ref, vb_ref, u_ref, *, sb_width, kv_transposed):
    x1 = _ffn_half(x_ref[...], g1_ref, wg_ref, wu_ref, wd_ref)
    x1_ref[...] = x1
    h = _rms(x1, gmix_ref[...]).astype(BF16)
    proj = _dot(h, win_ref[...])
    q = proj[:, :sb_width]
    k = proj[:, sb_width:2 * sb_width]
    v = proj[:, 2 * sb_width:3 * sb_width]
    if kv_transposed:
        k_ref[0] = k.T
        v_ref[0] = v.T
    else:
        k_ref[...] = k
        v_ref[...] = v
    qb_ref[...] = (q * SB_HEAD_DIM ** -0.5).astype(BF16)
    kb_ref[...] = k.astype(BF16)
    vb_ref[...] = v.astype(BF16)
    u_ref[...] = proj[:, 3 * sb_width:]


def _in_proj(x, g1, wg, wu, wd, gmix, win, *, tm, sb_width, u_shape, u_index, kv_time_tiles=None):
    n, d = x.shape
    dff = wg.shape[1]
    ssm_width = win.shape[1] - 3 * sb_width
    row = lambda w: pl.BlockSpec((tm, w), lambda i: (i, 0))
    if kv_time_tiles is None:
        kv_spec = row(sb_width)
        kv_shape = jax.ShapeDtypeStruct((n, sb_width), F32)
    else:
        t = kv_time_tiles
        kv_spec = pl.BlockSpec((1, sb_width, tm), lambda i: (i // t, 0, i % t))
        kv_shape = jax.ShapeDtypeStruct((n // (t * tm), sb_width, t * tm), F32)
    return pl.pallas_call(
        functools.partial(_in_proj_kernel, sb_width=sb_width, kv_transposed=kv_time_tiles is not None),
        grid=(n // tm,),
        in_specs=[row(d), _const_spec((1, d)), _const_spec((d, dff)), _const_spec((d, dff)),
                  _const_spec((dff, d)), _const_spec((1, d)), _const_spec(win.shape)],
        out_specs=[row(d), kv_spec, kv_spec, row(sb_width), row(sb_width), row(sb_width),
                   pl.BlockSpec((tm, ssm_width), u_index)],
        out_shape=[jax.ShapeDtypeStruct((n, d), F32), kv_shape, kv_shape,
                   jax.ShapeDtypeStruct((n, sb_width), BF16), jax.ShapeDtypeStruct((n, sb_width), BF16),
                   jax.ShapeDtypeStruct((n, sb_width), BF16),
                   jax.ShapeDtypeStruct(u_shape, F32)],
        compiler_params=_params("parallel"),
        name="in_proj",
    )(x, g1, wg, wu, wd, gmix, win)


def _sb_prompt_kernel(bias_ref, q_ref, k_ref, v_ref, o_ref, carry_ref, acc_ref, z_ref, w_ref, *, blk):
    pair = pl.program_id(1)
    nq = q_ref.shape[1] // blk
    lane = lax.broadcasted_iota(jnp.int32, (blk, LANES), 1)
    first = lane < SB_HEAD_DIM
    bias2 = jnp.concatenate(
        [jnp.full((blk, 1), bias_ref[2 * pair]), jnp.full((blk, 1), bias_ref[2 * pair + 1])], axis=0)
    suffix = _suffix_matrix(blk, BF16)

    def block_of(ref, j):
        return ref[0, pl.ds(pl.multiple_of(j * blk, blk), blk), :]

    def scores(qi, j):
        q = block_of(q_ref, qi)
        zero = jnp.zeros_like(q)
        q2 = jnp.concatenate([jnp.where(first, q, zero), jnp.where(first, zero, q)], axis=0)
        return _dot_nt(q2, block_of(k_ref, j))

    def finish_values(j_prev):
        return acc_ref[...] + _dot(w_ref[...], block_of(v_ref, j_prev))

    def write_out(qi, acc):
        o_ref[0, pl.ds(pl.multiple_of(qi * blk, blk), blk), :] = jnp.where(first, acc[:blk], acc[blk:])

    def weights(visible, carry):
        ls, lk = _log_sigmoid_pair(z_ref[...] + bias2)
        if visible is not None:
            lk = jnp.where(visible, lk, 0.0)
        w = jnp.exp(ls + _split_dot(lk, suffix) + carry)
        if visible is not None:
            w = jnp.where(visible, w, 0.0)
        return w.astype(BF16), jnp.sum(lk, axis=-1, keepdims=True)

    qpos = lax.broadcasted_iota(jnp.int32, (2 * blk, blk), 0)
    kpos = lax.broadcasted_iota(jnp.int32, (2 * blk, blk), 1)
    diagonal = kpos < jnp.where(qpos >= blk, qpos - blk, qpos)

    acc_ref[...] = jnp.zeros_like(acc_ref)
    w_ref[...] = jnp.zeros_like(w_ref)
    z_ref[...] = scores(0, 0)

    @pl.loop(0, nq)
    def _(qi):
        w, rowsum = weights(diagonal, 0.0)
        last = qi == nq - 1
        z_ref[...] = scores(jnp.where(qi == 0, jnp.where(last, 0, 1), qi),
                            jnp.where(qi == 0, jnp.where(last, 0, 1), qi - 1))
        write_out(jnp.maximum(qi - 1, 0), finish_values(0))
        acc_ref[...] = jnp.zeros_like(acc_ref)
        w_ref[...] = w
        carry_ref[...] = rowsum

        @pl.loop(0, qi)
        def _(i):
            j = qi - 1 - i
            carry = carry_ref[...]
            w, rowsum = weights(None, carry)
            end = j == 0
            nxt = jnp.minimum(qi + 1, nq - 1)
            z_ref[...] = scores(jnp.where(end, nxt, qi), jnp.where(end, nxt, j - 1))
            acc_ref[...] = finish_values(j + 1)
            w_ref[...] = w
            carry_ref[...] = carry + rowsum

    write_out(nq - 1, finish_values(0))


def _sb_prompt(bias, qb, kb, vb, *, blk):
    b, l, w = qb.shape
    seq_block = pl.BlockSpec((1, l, LANES), lambda bi, p: (bi, 0, p))
    return pl.pallas_call(
        functools.partial(_sb_prompt_kernel, blk=blk),
        grid=(b, w // LANES),
        in_specs=[pl.BlockSpec(memory_space=pltpu.SMEM), seq_block, seq_block, seq_block],
        out_specs=seq_block,
        out_shape=jax.ShapeDtypeStruct((b, l, w), F32),
        scratch_shapes=[pltpu.VMEM((2 * blk, 1), F32), pltpu.VMEM((2 * blk, LANES), F32),
                        pltpu.VMEM((2 * blk, blk), F32), pltpu.VMEM((2 * blk, blk), BF16)],
        compiler_params=_params("parallel", "parallel"),
        name="sb_prompt",
    )(bias, qb, kb, vb)


def _sc_mesh():
    return plsc.VectorSubcoreMesh(core_axis_name="c", subcore_axis_name="s", num_cores=SC_CORES, num_subcores=SC_SUBCORES)


def _sc_sequence_index():
    return lax.axis_index("s") * SC_CORES + lax.axis_index("c")


def _sb_sample_scores(kt_slabs, idx, q_lanes, *, heads):
    nb, ng, gk = idx.shape
    _, hd, page = kt_slabs.shape
    groups_per_head = ng // heads
    nv = page // SC_LANES

    @functools.partial(
        pl.kernel, mesh=_sc_mesh(),
        out_type=jax.ShapeDtypeStruct((nb, ng * gk, page), F32),
        scratch_types=[pltpu.VMEM((ng, gk), jnp.int32), pltpu.VMEM(q_lanes.shape[1:], F32),
                       pltpu.VMEM((2, gk, hd, page), F32), pltpu.VMEM((gk, page), F32),
                       pltpu.SemaphoreType.DMA((2,))])
    def kern(kt_hbm, idx_hbm, q_hbm, z_hbm, idx_v, q_v, buf, zrow, sems):
        b = _sc_sequence_index()
        pltpu.sync_copy(idx_hbm.at[b], idx_v)
        pltpu.sync_copy(q_hbm.at[b], q_v)

        def gather(g, slot):
            return pltpu.make_async_copy(kt_hbm.at[idx_v.at[g]], buf.at[slot], sems.at[slot])

        gather(0, 0).start()

        @pl.loop(0, ng, step=2)
        def _(g0):
            for slot in range(2):
                g = g0 + slot

                @pl.when(g + 1 < ng)
                def _():
                    gather(g + 1, 1 - slot).start()

                gather(g, slot).wait()
                h = g // groups_per_head
                for i in range(gk):
                    for v in range(nv):
                        zrow[i, pl.ds(v * SC_LANES, SC_LANES)] = jnp.zeros((SC_LANES,), F32)

                    @pl.loop(0, hd, step=SC_DIM_CHUNK)
                    def _(d0):
                        accs = [None] * nv
                        for dd in range(SC_DIM_CHUNK):
                            qv = q_v[h * (hd // SC_DIM_CHUNK) + d0 // SC_DIM_CHUNK, pl.ds(dd * SC_LANES, SC_LANES)]
                            for v in range(nv):
                                t = qv * buf[slot, i, d0 + dd, pl.ds(v * SC_LANES, SC_LANES)]
                                accs[v] = t if accs[v] is None else accs[v] + t
                        for v in range(nv):
                            plsc.addupdate(zrow.at[i, pl.ds(v * SC_LANES, SC_LANES)], accs[v])
                pltpu.sync_copy(zrow, z_hbm.at[b, pl.ds(g * gk, gk)])

    return kern(kt_slabs, idx, q_lanes)


def _sb_sample_values(vt_slabs, idx, w, *, heads):
    nb, ng, gk = idx.shape
    _, hd, page = vt_slabs.shape
    groups_per_head = ng // heads
    nv = page // SC_LANES
    acc_rows = hd // SC_DIM_CHUNK

    @functools.partial(
        pl.kernel, mesh=_sc_mesh(),
        out_type=jax.ShapeDtypeStruct((nb, heads, acc_rows, page), F32),
        scratch_types=[pltpu.VMEM((ng, gk), jnp.int32), pltpu.VMEM((2, gk, hd, page), F32),
                       pltpu.VMEM((2, gk, page), F32), pltpu.VMEM((acc_rows, page), F32),
                       pltpu.SemaphoreType.DMA((2,)), pltpu.SemaphoreType.DMA((2,))])
    def kern(vt_hbm, idx_hbm, w_hbm, o_hbm, idx_v, buf, wbuf, acc, sems, wsems):
        b = _sc_sequence_index()
        pltpu.sync_copy(idx_hbm.at[b], idx_v)

        def copies(g, slot):
            return (pltpu.make_async_copy(vt_hbm.at[idx_v.at[g]], buf.at[slot], sems.at[slot]),
                    pltpu.make_async_copy(w_hbm.at[b, pl.ds(g * gk, gk)], wbuf.at[slot], wsems.at[slot]))

        for c in copies(0, 0):
            c.start()

        @pl.loop(0, heads)
        def _(h):
            for r in range(acc_rows):
                for v in range(nv):
                    acc[r, pl.ds(v * SC_LANES, SC_LANES)] = jnp.zeros((SC_LANES,), F32)

            @pl.loop(0, groups_per_head, step=2)
            def _(gg):
                for slot in range(2):
                    g = h * groups_per_head + gg + slot

                    @pl.when(g + 1 < ng)
                    def _():
                        for c in copies(g + 1, 1 - slot):
                            c.start()

                    for c in copies(g, slot):
                        c.wait()
                    for i in range(gk):
                        @pl.loop(0, hd, step=SC_DIM_CHUNK)
                        def _(d0):
                            wv = [wbuf[slot, i, pl.ds(v * SC_LANES, SC_LANES)] for v in range(nv)]
                            for dd in range(SC_DIM_CHUNK):
                                t = wv[0] * buf[slot, i, d0 + dd, pl.ds(0, SC_LANES)]
                                for v in range(1, nv):
                                    t = t + wv[v] * buf[slot, i, d0 + dd, pl.ds(v * SC_LANES, SC_LANES)]
                                plsc.addupdate(acc.at[d0 // SC_DIM_CHUNK, pl.ds(dd * SC_LANES, SC_LANES)], t)

            pltpu.sync_copy(acc, o_hbm.at[b, h])

    return kern(vt_slabs, idx, w)


def _sb_sample_weights_kernel(z_ref, q_ref, knew_ref, vnew_ref, bias_ref, _order_ref, w_ref, onew_ref, carry, *,
                              new_key_visible):
    bias = bias_ref[...]

    @pl.when(pl.program_id(0) == 0)
    def _():
        z_new = jnp.sum(q_ref[...] * knew_ref[...], axis=-1, keepdims=True) + bias
        ls_new, lk_new = _log_sigmoid_pair(z_new)
        vis = jnp.full(z_new.shape, new_key_visible)
        carry[...] = jnp.where(vis, lk_new, 0.0)
        onew_ref[...] = jnp.where(vis, jnp.exp(ls_new), 0.0) * vnew_ref[...]

    ls, lk = _log_sigmoid_pair(z_ref[...] + bias)
    suffix = _suffix_matrix(MXU_DIM, BF16)
    run = carry[...]
    after = []
    for c in reversed(range(z_ref.shape[1] // MXU_DIM)):
        lkc = lk[:, c * MXU_DIM:(c + 1) * MXU_DIM]
        after.append(_split_dot(lkc, suffix) + run)
        run = run + jnp.sum(lkc, axis=-1, keepdims=True)
    carry[...] = run
    w_ref[...] = jnp.exp(ls + jnp.concatenate(after[::-1], axis=1))


def _sb_sample_weights(z, q, k_new, v_new, bias_rows, run_after, *, new_key_visible):
    rows, keys = z.shape
    hd = q.shape[1]
    chunk = SB_SAMPLE_KEY_CHUNK
    nsteps = keys // chunk
    zspec = pl.BlockSpec((rows, chunk), lambda c: (0, nsteps - 1 - c))
    small = lambda w: pl.BlockSpec((rows, w), lambda c: (0, 0))
    return pl.pallas_call(
        functools.partial(_sb_sample_weights_kernel, new_key_visible=new_key_visible),
        grid=(nsteps,),
        in_specs=[zspec, small(hd), small(hd), small(hd), small(1),
                  pl.BlockSpec(run_after.shape, lambda c: (0,) * run_after.ndim)],
        out_specs=[zspec, small(hd)],
        out_shape=[jax.ShapeDtypeStruct((rows, keys), F32), jax.ShapeDtypeStruct((rows, hd), F32)],
        scratch_shapes=[pltpu.VMEM((rows, 1), F32)],
        compiler_params=_params("arbitrary"),
        name="sb_sample_weights",
    )(z, q, k_new, v_new, bias_rows, run_after)


def _sb_sample_finish_kernel(parts_ref, onew_ref, o_ref):
    lanes = parts_ref.shape[1]
    j = lax.broadcasted_iota(jnp.int32, (lanes, lanes), 0)
    c = lax.broadcasted_iota(jnp.int32, (lanes, lanes), 1)
    group_sum = jnp.where(j // SC_LANES == c, 1.0, 0.0).astype(BF16)
    o_ref[...] = _split_dot(parts_ref[...], group_sum)[:, :o_ref.shape[1]] + onew_ref[...]


def _sb_sample_finish(parts, o_new):
    rows, lanes = parts.shape
    cols = o_new.shape[1]
    return pl.pallas_call(
        _sb_sample_finish_kernel,
        grid=(1,),
        in_specs=[pl.BlockSpec((rows, lanes), lambda i: (0, 0)), pl.BlockSpec((rows, cols), lambda i: (0, 0))],
        out_specs=pl.BlockSpec((rows, cols), lambda i: (0, 0)),
        out_shape=jax.ShapeDtypeStruct((rows, cols), F32),
        compiler_params=_params("arbitrary"),
        name="sb_sample_finish",
    )(parts, o_new)


def _sb_sample(page_table, q, k_new, v_new, bias, cache_k, cache_v, weights_after, *, new_key_visible):
    nb, n_pages = page_table.shape
    n_pool, page, heads, hd = cache_k.shape
    assert nb == SC_CORES * SC_SUBCORES and page % SC_LANES == 0 and hd % SC_DIM_CHUNK == 0
    assert page == SC_DIM_CHUNK * SC_LANES and n_pages % (2 * SC_GATHER_SLABS) == 0
    slabs = lambda c: c.transpose(0, 2, 3, 1).reshape(n_pool * heads, hd, page)
    idx = (page_table[:, None, :] * heads + jnp.arange(heads, dtype=jnp.int32)[None, :, None]).reshape(
        nb, heads * n_pages // SC_GATHER_SLABS, SC_GATHER_SLABS)
    q_lanes = jnp.repeat(q, SC_LANES, axis=-1).reshape(nb, heads * hd // SC_DIM_CHUNK, page)
    z = _sb_sample_scores(slabs(cache_k), idx, q_lanes, heads=heads)
    per_head = lambda a: a.reshape(nb * heads, hd)
    w, o_new = _sb_sample_weights(z.reshape(nb * heads, n_pages * page), per_head(q), per_head(k_new), per_head(v_new),
                                  jnp.tile(bias, nb).reshape(nb * heads, 1), weights_after,
                                  new_key_visible=new_key_visible)
    parts = _sb_sample_values(slabs(cache_v), idx, w.reshape(nb, heads * n_pages, page), heads=heads)
    o = _sb_sample_finish(parts.reshape(nb * heads * hd // SC_DIM_CHUNK, page),
                          o_new.reshape(nb * heads * hd // SC_DIM_CHUNK, SC_DIM_CHUNK))
    return o.reshape(nb, heads * hd)


def _ssm_prep_kernel(are_ref, aim_ref, ldt_ref, bre_ref, bim_ref, bcat_ref, abr_ref, abi_ref):
    dt = jnp.exp(ldt_ref[...])
    lam_re = jnp.minimum(are_ref[...], A_RE_CLIP)
    lam_im = aim_ref[...]
    mag = jnp.exp(lam_re * dt)
    ab_re = mag * jnp.cos(lam_im * dt)
    ab_im = mag * jnp.sin(lam_im * dt)
    den = lam_re * lam_re + lam_im * lam_im
    nr = ab_re - 1.0
    f_re = (nr * lam_re + ab_im * lam_im) / den
    f_im = (ab_im * lam_re - nr * lam_im) / den
    abr_ref[...] = ab_re
    abi_ref[...] = ab_im
    br, bi = bre_ref[...], bim_ref[...]
    gp = br.shape[1]
    bcat_ref[:, :gp] = (f_re * br - f_im * bi).astype(BF16)
    bcat_ref[:, gp:] = (f_re * bi + f_im * br).astype(BF16)


def _ssm_prep(a_re, a_im, log_dt_full, b_re_bd, b_im_bd):
    gc, gp = b_re_bd.shape
    vec = _const_spec((1, gp))
    return pl.pallas_call(
        _ssm_prep_kernel,
        grid=(1,),
        in_specs=[vec, vec, vec, _const_spec((gc, gp)), _const_spec((gc, gp))],
        out_specs=[pl.BlockSpec((gc, 2 * gp), lambda i: (0, 0)),
                   pl.BlockSpec((1, gp), lambda i: (0, 0)), pl.BlockSpec((1, gp), lambda i: (0, 0))],
        out_shape=[jax.ShapeDtypeStruct((gc, 2 * gp), BF16),
                   jax.ShapeDtypeStruct((1, gp), F32), jax.ShapeDtypeStruct((1, gp), F32)],
        compiler_params=_params("arbitrary"),
        name="ssm_prep",
    )(a_re, a_im, log_dt_full, b_re_bd, b_im_bd)


def _gelu_tanh(y):
    return 0.5 * y * (1.0 + jnp.tanh(math.sqrt(2.0 / math.pi) * (y + 0.044715 * (y * y * y))))


def _ssm_kernel(*refs, steps, nb, has_init):
    (u_ref, bcat_ref, cre_ref, cim_ref, abr_ref, abi_ref, d_ref, wglu_ref, bglu_ref), refs = refs[:9], refs[9:]
    if has_init:
        (s0re_ref, s0im_ref), refs = refs[:2], refs[2:]
    o_ref, sfre_ref, sfim_ref, sre, sim, st_re, st_im = refs
    c = pl.program_id(0)
    gp = sre.shape[1]

    @pl.when(c == 0)
    def _():
        if has_init:
            st_re[...] = s0re_ref[...]
            st_im[...] = s0im_ref[...]
        else:
            st_re[...] = jnp.zeros_like(st_re)
            st_im[...] = jnp.zeros_like(st_im)

    u = u_ref[...]
    bu = _dot(u.astype(BF16), bcat_ref[...])
    sre[...] = bu[:, :gp]
    sim[...] = bu[:, gp:]
    a_re = jnp.broadcast_to(abr_ref[...], (nb, gp))
    a_im = jnp.broadcast_to(abi_ref[...], (nb, gp))

    def step(t, state):
        s_re, s_im = state
        rows = pl.ds(pl.multiple_of(t * nb, nb), nb)
        n_re = a_re * s_re - a_im * s_im + sre[rows, :]
        n_im = a_re * s_im + a_im * s_re + sim[rows, :]
        sre[rows, :] = n_re
        sim[rows, :] = n_im
        return n_re, n_im

    s_re, s_im = lax.fori_loop(0, steps, step, (st_re[...], st_im[...]))
    st_re[...] = s_re
    st_im[...] = s_im

    y = _dot(sre[...].astype(BF16), cre_ref[...]) - _dot(sim[...].astype(BF16), cim_ref[...])
    y = y + d_ref[...] * u
    g = _gelu_tanh(y)
    o_ref[...] = g * jax.nn.sigmoid(_dot(g.astype(BF16), wglu_ref[...]) + bglu_ref[...])

    @pl.when(c == pl.num_programs(0) - 1)
    def _():
        sfre_ref[...] = s_re
        sfim_ref[...] = s_im


def _ssm(u_tb, bcat, cre_bd, cim_bd, ab_re, ab_im, d_skip, w_glu, b_glu, s0, *, steps, nb):
    rows, width = u_tb.shape
    gp = ab_re.shape[1]
    chunk = steps * nb
    has_init = s0 is not None
    tile = pl.BlockSpec((chunk, width), lambda c: (c, 0))
    state = _const_spec((nb, gp))
    in_specs = [tile, _const_spec(bcat.shape), _const_spec(cre_bd.shape), _const_spec(cim_bd.shape),
                _const_spec((1, gp)), _const_spec((1, gp)), _const_spec((1, width)),
                _const_spec(w_glu.shape), _const_spec((1, width))]
    args = [u_tb, bcat, cre_bd, cim_bd, ab_re, ab_im, d_skip, w_glu, b_glu]
    if has_init:
        in_specs += [state, state]
        args += list(s0)
    return pl.pallas_call(
        functools.partial(_ssm_kernel, steps=steps, nb=nb, has_init=has_init),
        grid=(rows // chunk,),
        in_specs=in_specs,
        out_specs=[tile, pl.BlockSpec((nb, gp), lambda c: (0, 0)), pl.BlockSpec((nb, gp), lambda c: (0, 0))],
        out_shape=[jax.ShapeDtypeStruct((rows, width), F32),
                   jax.ShapeDtypeStruct((nb, gp), F32), jax.ShapeDtypeStruct((nb, gp), F32)],
        scratch_shapes=[pltpu.VMEM((chunk, gp), F32), pltpu.VMEM((chunk, gp), F32),
                        pltpu.VMEM((nb, gp), F32), pltpu.VMEM((nb, gp), F32)],
        compiler_params=_params("arbitrary"),
        name="ssm",
    )(*args)


def _mem_kv_kernel(m_ref, g_ref, wk_ref, wv_ref, k_ref, v_ref, kb_ref, vb_ref):
    h = _rms(m_ref[...], g_ref[...]).astype(BF16)
    k = _dot(h, wk_ref[...])
    v = _dot(h, wv_ref[...])
    k_ref[...] = k
    v_ref[...] = v
    kb_ref[...] = k.astype(BF16)
    vb_ref[...] = v.astype(BF16)


def _mem_kv(mem, g, wk, wv, *, tm):
    n, d = mem.shape
    row = pl.BlockSpec((tm, d), lambda i: (i, 0))
    return pl.pallas_call(
        _mem_kv_kernel,
        grid=(n // tm,),
        in_specs=[row, _const_spec((1, d)), _const_spec(wk.shape), _const_spec(wv.shape)],
        out_specs=[row] * 4,
        out_shape=[jax.ShapeDtypeStruct((n, d), F32)] * 2 + [jax.ShapeDtypeStruct((n, d), BF16)] * 2,
        compiler_params=_params("parallel"),
        name="mem_kv",
    )(mem, g, wk, wv)


def _mix_out_kernel(x1_ref, osb_ref, ossm_ref, gsb_ref, gssm_ref, wout_ref, gx_ref, wxq_ref, x2_ref, qx_ref, *, xscale):
    mixed = jnp.concatenate([_rms(osb_ref[...], gsb_ref[...]), _rms(ossm_ref[...], gssm_ref[...])], axis=-1)
    x2 = x1_ref[...] + _dot(mixed.astype(BF16), wout_ref[...])
    x2_ref[...] = x2
    hx = _rms(x2, gx_ref[...]).astype(BF16)
    qx_ref[...] = (_dot(hx, wxq_ref[...]) * xscale).astype(BF16)


def _mix_out(x1, o_sb, o_ssm, g_sb, g_ssm, w_out, g_x, w_xq, *, tm, ssm_index, xscale):
    n, d = x1.shape
    sbw = o_sb.shape[1]
    ssw = g_ssm.shape[1]
    row = lambda w: pl.BlockSpec((tm, w), lambda i: (i, 0))
    return pl.pallas_call(
        functools.partial(_mix_out_kernel, xscale=xscale),
        grid=(n // tm,),
        in_specs=[row(d), row(sbw), pl.BlockSpec((tm, ssw), ssm_index), _const_spec((1, sbw)), _const_spec((1, ssw)),
                  _const_spec(w_out.shape), _const_spec((1, d)), _const_spec(w_xq.shape)],
        out_specs=[row(d), row(d)],
        out_shape=[jax.ShapeDtypeStruct((n, d), F32), jax.ShapeDtypeStruct((n, d), BF16)],
        compiler_params=_params("parallel"),
        name="mix_out",
    )(x1, o_sb, o_ssm, g_sb, g_ssm, w_out, g_x, w_xq)


def _xattn_kernel(q_ref, k_ref, v_ref, o_ref, *, rows):
    q = q_ref[0]
    if q.shape[0] != rows:
        q = jnp.broadcast_to(q, (rows, q.shape[1]))
    hd = q.shape[1] // X_HEADS

    def mem_head(ref, h):
        return ref[0, :, h * hd:(h + 1) * hd].astype(BF16)

    outs = []
    for h in range(X_HEADS):
        s = _dot_nt(q[:, h * hd:(h + 1) * hd], mem_head(k_ref, h))
        e = jnp.exp(s - jnp.max(s, axis=-1, keepdims=True))
        p = e / jnp.sum(e, axis=-1, keepdims=True)
        outs.append(_dot(p.astype(BF16), mem_head(v_ref, h)))
    o = jnp.concatenate(outs, axis=-1).astype(o_ref.dtype)
    o_ref[0] = o[:o_ref.shape[1]]


def _xattn(q, mem_k, mem_v, *, tq):
    b, l, d = q.shape
    m = mem_k.shape[1]
    rows = max(tq, 16)
    qspec = pl.BlockSpec((1, tq, d), lambda bi, i: (bi, i, 0))
    mspec = pl.BlockSpec((1, m, d), lambda bi, i: (bi, 0, 0))
    return pl.pallas_call(
        functools.partial(_xattn_kernel, rows=rows),
        grid=(b, l // tq),
        in_specs=[qspec, mspec, mspec],
        out_specs=qspec,
        out_shape=jax.ShapeDtypeStruct((b, l, d), BF16),
        compiler_params=_params("parallel", "arbitrary"),
        name="xattn",
    )(q, mem_k, mem_v)


def _out_kernel(x2_ref, ox_ref, wxo_ref, g2_ref, wg_ref, wu_ref, wd_ref, gfin_ref, y_ref):
    x3 = x2_ref[...] + _dot(ox_ref[...], wxo_ref[...])
    x4 = _ffn_half(x3, g2_ref, wg_ref, wu_ref, wd_ref)
    y_ref[...] = _rms(x4, gfin_ref[...])


def _out(x2, ox, w_xo, g2, wg, wu, wd, g_fin, *, tm):
    n, d = x2.shape
    dff = wg.shape[1]
    row = pl.BlockSpec((tm, d), lambda i: (i, 0))
    return pl.pallas_call(
        _out_kernel,
        grid=(n // tm,),
        in_specs=[row, row, _const_spec(w_xo.shape), _const_spec((1, d)), _const_spec((d, dff)),
                  _const_spec((d, dff)), _const_spec((dff, d)), _const_spec((1, d))],
        out_specs=row,
        out_shape=jax.ShapeDtypeStruct((n, d), F32),
        compiler_params=_params("parallel"),
        name="ffn2_out",
    )(x2, ox, w_xo, g2, wg, wu, wd, g_fin)


def _block_diag(x):
    g, r, c = x.shape
    eye = jnp.eye(g, dtype=x.dtype)
    return (x[:, :, None, :] * eye[:, None, :, None]).reshape(g * r, g * c)


def kernel(x_prompt, x_sample, cache_k, cache_v, state_ssm_re, state_ssm_im, cache_mem_k, cache_mem_v, page_table, mem_prompt, g_ffn1, w_ffn1_gate, w_ffn1_up, w_ffn1_down, g_mix, w_in, sb_bias, ssm_a_re, ssm_a_im, ssm_log_dt, ssm_b_re, ssm_b_im, ssm_c_re, ssm_c_im, ssm_d, w_glu, b_glu, g_sb_out, g_ssm_out, w_out, g_xattn, g_mem, w_xq, w_xk, w_xv, w_xo, g_ffn2, w_ffn2_gate, w_ffn2_up, w_ffn2_down, g_final):
    depth = w_in.shape[0]
    assert depth == 1, "single-layer step"
    bsz, seq, d = x_prompt.shape
    dec_b, dec_seq, _ = x_sample.shape
    assert dec_seq == 1
    n_pool, page, sb_heads, hd = cache_k.shape[1:]
    assert hd == SB_HEAD_DIM
    sbw = sb_heads * hd
    groups, states = ssm_a_re.shape[1:]
    gp = groups * states
    ssw = groups * SSM_GROUP
    n_mem = mem_prompt.shape[1]
    past_len = page_table.shape[1] * page
    xscale = (d // X_HEADS) ** -0.5

    bf = lambda w: w[0].astype(BF16)
    vec = lambda g: g.reshape(1, -1)
    ffn1 = (vec(g_ffn1[0]), bf(w_ffn1_gate), bf(w_ffn1_up), bf(w_ffn1_down))
    ffn2 = (vec(g_ffn2[0]), bf(w_ffn2_gate), bf(w_ffn2_up), bf(w_ffn2_down))
    w_in_b, w_out_b, w_xq_b, w_xo_b = bf(w_in), bf(w_out), bf(w_xq), bf(w_xo)

    bcat, ab_re, ab_im = _ssm_prep(
        ssm_a_re[0].reshape(1, gp), ssm_a_im[0].reshape(1, gp),
        jnp.repeat(ssm_log_dt[0], states).reshape(1, gp),
        _block_diag(ssm_b_re[0].transpose(0, 2, 1)), _block_diag(ssm_b_im[0].transpose(0, 2, 1)))
    cre_bd = _block_diag(ssm_c_re[0].transpose(0, 2, 1)).astype(BF16)
    cim_bd = _block_diag(ssm_c_im[0].transpose(0, 2, 1)).astype(BF16)
    ssm_w = (bcat, cre_bd, cim_bd, ab_re, ab_im, vec(ssm_d[0]), bf(w_glu), vec(b_glu[0]))

    row0 = lambda i: (0, 0)
    x1s, k_s, v_s, qbs, _, _, u_s = _in_proj(
        x_sample.reshape(dec_b, d), *ffn1, vec(g_mix[0]), w_in_b,
        tm=dec_b, sb_width=sbw, u_shape=(dec_b, ssw), u_index=row0)

    tm = PROMPT_TOKEN_TILE
    tiles = seq // tm
    tb_index = lambda i: (i % tiles, i // tiles)
    n = bsz * seq
    x1, k_p, v_p, qb, kb, vb, u_tb = _in_proj(
        x_prompt.reshape(n, d), *ffn1, vec(g_mix[0]), w_in_b,
        tm=tm, sb_width=sbw, u_shape=(seq, bsz * ssw), u_index=tb_index, kv_time_tiles=tiles)
    o_sb = _sb_prompt(sb_bias[0], qb.reshape(bsz, seq, sbw), kb.reshape(bsz, seq, sbw), vb.reshape(bsz, seq, sbw),
                      blk=SB_BLOCK)
    o_ssm, sre_p, sim_p = _ssm(u_tb.reshape(seq * bsz, ssw), *ssm_w, None, steps=SSM_CHUNK_STEPS, nb=bsz)
    o_sb_s = _sb_sample(page_table, qbs.astype(F32), k_s, v_s, sb_bias[0], cache_k[0], cache_v[0], sre_p,
                        new_key_visible=bool(past_len < past_len))
    mk_p, mv_p, mkb, mvb = _mem_kv(mem_prompt.reshape(bsz * n_mem, d), vec(g_mem[0]), bf(w_xk), bf(w_xv), tm=tm)
    x2, qx = _mix_out(x1, o_sb.reshape(n, sbw), o_ssm.reshape(seq, bsz * ssw), vec(g_sb_out[0]), vec(g_ssm_out[0]),
                      w_out_b, vec(g_xattn[0]), w_xq_b, tm=tm, ssm_index=tb_index, xscale=xscale)
    ox = _xattn(qx.reshape(bsz, seq, d), mkb.reshape(bsz, n_mem, d), mvb.reshape(bsz, n_mem, d), tq=XATTN_Q_TILE)
    y_p = _out(x2, ox.reshape(n, d), w_xo_b, *ffn2, vec(g_final), tm=tm)

    o_ssm_s, sre_s, sim_s = _ssm(u_s, *ssm_w, (state_ssm_re[0].reshape(dec_b, gp), state_ssm_im[0].reshape(dec_b, gp)),
                                 steps=1, nb=dec_b)
    x2s, qxs = _mix_out(x1s, o_sb_s, o_ssm_s, vec(g_sb_out[0]), vec(g_ssm_out[0]),
                        w_out_b, vec(g_xattn[0]), w_xq_b, tm=dec_b, ssm_index=row0, xscale=xscale)
    mem_s = lambda c: c[0].reshape(dec_b, n_mem, d)
    oxs = _xattn(qxs.reshape(dec_b, 1, d), mem_s(cache_mem_k), mem_s(cache_mem_v), tq=1)
    y_s = _out(x2s, oxs.reshape(dec_b, d), w_xo_b, *ffn2, vec(g_final), tm=dec_b)

    kv = lambda a, b_, l_: a.reshape(1, b_, l_, sb_heads, hd)
    kv_t = lambda a: a.reshape(1, bsz, sb_heads, hd, seq).transpose(0, 1, 4, 2, 3)
    st = lambda a, b_: a.reshape(1, b_, groups, states)
    mem = lambda a: a.reshape(1, bsz, n_mem, X_HEADS, d // X_HEADS)
    return (y_p.reshape(bsz, seq, d), y_s.reshape(dec_b, 1, d),
            kv_t(k_p), kv_t(v_p), kv(k_s, dec_b, 1), kv(v_s, dec_b, 1),
            st(sre_p, bsz), st(sim_p, bsz), st(sre_s, dec_b), st(sim_s, dec_b),
            mem(mk_p), mem(mv_p))
```

```python
import functools
import math

import jax
import jax.numpy as jnp
from jax import lax
from jax.experimental import pallas as pl
from jax.experimental.pallas import tpu as pltpu
from jax.experimental.pallas import tpu_sc as plsc

F32 = jnp.float32
BF16 = jnp.bfloat16

EPS = 1e-6
A_RE_CLIP = -1e-4
SB_HEAD_DIM = 64
SSM_GROUP = 16
X_HEADS = 4

LANES = 128
MXU_DIM = 256
VMEM_LIMIT_BYTES = 56 * 1024 * 1024
SC_CORES = 2
SC_SUBCORES = 16
SC_LANES = 16
SC_GATHER_SLABS = 4
SC_DIM_CHUNK = 8

PROMPT_TOKEN_TILE = 256
SB_BLOCK = 256
SSM_CHUNK_STEPS = 64
SSM_SCAN_UNROLL = 4
SB_SAMPLE_KEY_CHUNK = 1024
XATTN_Q_TILE = 512


def _params(*semantics):
    return pltpu.CompilerParams(dimension_semantics=semantics, vmem_limit_bytes=VMEM_LIMIT_BYTES)


def _const_spec(shape):
    return pl.BlockSpec(shape, lambda *_: (0,) * len(shape), pipeline_mode=pl.Buffered(1))


def _dot(a, b):
    return jnp.dot(a, b, preferred_element_type=F32)


def _dot_nt(a, b):
    return lax.dot_general(a, b, (((1,), (1,)), ((), ())), preferred_element_type=F32)


def _rms(x, g):
    return x * lax.rsqrt(jnp.mean(x * x, axis=-1, keepdims=True) + EPS) * g


def _ffn_half(x, g_ref, wg_ref, wu_ref, wd_ref):
    h = _rms(x, g_ref[...]).astype(BF16)
    gate = _dot(h, wg_ref[...])
    up = _dot(h, wu_ref[...])
    act = (gate * jax.nn.sigmoid(gate) * up).astype(BF16)
    return x + 0.5 * _dot(act, wd_ref[...])


def _log_sigmoid_pair(z):
    ls = jnp.minimum(z, 0.0) - jnp.log(1.0 + jnp.exp(-jnp.abs(z)))
    return ls, ls - z


def _suffix_matrix(n, dtype):
    j = lax.broadcasted_iota(jnp.int32, (n, n), 0)
    s = lax.broadcasted_iota(jnp.int32, (n, n), 1)
    return jnp.where(j > s, 1.0, 0.0).astype(dtype)


def _split_dot(x, m):
    hi = x.astype(BF16)
    lo = (x - hi.astype(F32)).astype(BF16)
    return _dot(hi, m) + _dot(lo, m)


def _in_proj_kernel(x_ref, g1_ref, wg_ref, wu_ref, wd_ref, gmix_ref, win_ref,
                    x1_ref, k_ref, v_ref, qb_ref, kb_ref, vb_ref, u_ref, *, sb_width, kv_transposed):
    x1 = _ffn_half(x_ref[...], g1_ref, wg_ref, wu_ref, wd_ref)
    x1_ref[...] = x1
    h = _rms(x1, gmix_ref[...]).astype(BF16)
    proj = _dot(h, win_ref[...])
    q = proj[:, :sb_width]
    k = proj[:, sb_width:2 * sb_width]
    v = proj[:, 2 * sb_width:3 * sb_width]
    if kv_transposed:
        k_ref[0] = k.T
        v_ref[0] = v.T
    else:
        k_ref[...] = k
        v_ref[...] = v
    qb_ref[...] = (q * SB_HEAD_DIM ** -0.5).astype(BF16)
    kb_ref[...] = k.astype(BF16)
    vb_ref[...] = v.astype(BF16)
    u_ref[...] = proj[:, 3 * sb_width:]


def _in_proj(x, g1, wg, wu, wd, gmix, win, *, tm, sb_width, u_shape, u_index, kv_time_tiles=None):
    n, d = x.shape
    dff = wg.shape[1]
    ssm_width = win.shape[1] - 3 * sb_width
    row = lambda w: pl.BlockSpec((tm, w), lambda i: (i, 0))
    if kv_time_tiles is None:
        kv_spec = row(sb_width)
        kv_shape = jax.ShapeDtypeStruct((n, sb_width), F32)
    else:
        t = kv_time_tiles
        kv_spec = pl.BlockSpec((1, sb_width, tm), lambda i: (i // t, 0, i % t))
        kv_shape = jax.ShapeDtypeStruct((n // (t * tm), sb_width, t * tm), F32)
    return pl.pallas_call(
        functools.partial(_in_proj_kernel, sb_width=sb_width, kv_transposed=kv_time_tiles is not None),
        grid=(n // tm,),
        in_specs=[row(d), _const_spec((1, d)), _const_spec((d, dff)), _const_spec((d, dff)),
                  _const_spec((dff, d)), _const_spec((1, d)), _const_spec(win.shape)],
        out_specs=[row(d), kv_spec, kv_spec, row(sb_width), row(sb_width), row(sb_width),
                   pl.BlockSpec((tm, ssm_width), u_index)],
        out_shape=[jax.ShapeDtypeStruct((n, d), F32), kv_shape, kv_shape,
                   jax.ShapeDtypeStruct((n, sb_width), BF16), jax.ShapeDtypeStruct((n, sb_width), BF16),
                   jax.ShapeDtypeStruct((n, sb_width), BF16),
                   jax.ShapeDtypeStruct(u_shape, F32)],
        compiler_params=_params("parallel"),
        name="in_proj",
    )(x, g1, wg, wu, wd, gmix, win)


def _sb_prompt_kernel(bias_ref, q_ref, k_ref, v_ref, o_ref, carry_ref, acc_ref, z_ref, w_ref, *, blk):
    pair = pl.program_id(1)
    nq = q_ref.shape[1] // blk
    lane = lax.broadcasted_iota(jnp.int32, (blk, LANES), 1)
    first = lane < SB_HEAD_DIM
    bias2 = jnp.concatenate(
        [jnp.full((blk, 1), bias_ref[2 * pair]), jnp.full((blk, 1), bias_ref[2 * pair + 1])], axis=0)
    suffix = _suffix_matrix(blk, BF16)

    def block_of(ref, j):
        return ref[0, pl.ds(pl.multiple_of(j * blk, blk), blk), :]

    def scores(qi, j):
        q = block_of(q_ref, qi)
        zero = jnp.zeros_like(q)
        q2 = jnp.concatenate([jnp.where(first, q, zero), jnp.where(first, zero, q)], axis=0)
        return _dot_nt(q2, block_of(k_ref, j))

    def finish_values(j_prev):
        return acc_ref[...] + _dot(w_ref[...], block_of(v_ref, j_prev))

    def write_out(qi, acc):
        o_ref[0, pl.ds(pl.multiple_of(qi * blk, blk), blk), :] = jnp.where(first, acc[:blk], acc[blk:])

    def weights(visible, carry):
        ls, lk = _log_sigmoid_pair(z_ref[...] + bias2)
        if visible is not None:
            lk = jnp.where(visible, lk, 0.0)
        w = jnp.exp(ls + _split_dot(lk, suffix) + carry)
        if visible is not None:
            w = jnp.where(visible, w, 0.0)
        return w.astype(BF16), jnp.sum(lk, axis=-1, keepdims=True)

    qpos = lax.broadcasted_iota(jnp.int32, (2 * blk, blk), 0)
    kpos = lax.broadcasted_iota(jnp.int32, (2 * blk, blk), 1)
    diagonal = kpos < jnp.where(qpos >= blk, qpos - blk, qpos)

    acc_ref[...] = jnp.zeros_like(acc_ref)
    w_ref[...] = jnp.zeros_like(w_ref)
    z_ref[...] = scores(0, 0)

    @pl.loop(0, nq)
    def _(qi):
        w, rowsum = weights(diagonal, 0.0)
        last = qi == nq - 1
        z_ref[...] = scores(jnp.where(qi == 0, jnp.where(last, 0, 1), qi),
                            jnp.where(qi == 0, jnp.where(last, 0, 1), qi - 1))
        write_out(jnp.maximum(qi - 1, 0), finish_values(0))
        acc_ref[...] = jnp.zeros_like(acc_ref)
        w_ref[...] = w
        carry_ref[...] = rowsum

        @pl.loop(0, qi)
        def _(i):
            j = qi - 1 - i
            carry = carry_ref[...]
            w, rowsum = weights(None, carry)
            end = j == 0
            nxt = jnp.minimum(qi + 1, nq - 1)
            z_ref[...] = scores(jnp.where(end, nxt, qi), jnp.where(end, nxt, j - 1))
            acc_ref[...] = finish_values(j + 1)
            w_ref[...] = w
            carry_ref[...] = carry + rowsum

    write_out(nq - 1, finish_values(0))


def _sb_prompt(bias, qb, kb, vb, *, blk):
    b, l, w = qb.shape
    seq_block = pl.BlockSpec((1, l, LANES), lambda bi, p: (bi, 0, p))
    return pl.pallas_call(
        functools.partial(_sb_prompt_kernel, blk=blk),
        grid=(b, w // LANES),
        in_specs=[pl.BlockSpec(memory_space=pltpu.SMEM), seq_block, seq_block, seq_block],
        out_specs=seq_block,
        out_shape=jax.ShapeDtypeStruct((b, l, w), F32),
        scratch_shapes=[pltpu.VMEM((2 * blk, 1), F32), pltpu.VMEM((2 * blk, LANES), F32),
                        pltpu.VMEM((2 * blk, blk), F32), pltpu.VMEM((2 * blk, blk), BF16)],
        compiler_params=_params("parallel", "parallel"),
        name="sb_prompt",
    )(bias, qb, kb, vb)


def _sc_mesh():
    return plsc.VectorSubcoreMesh(core_axis_name="c", subcore_axis_name="s", num_cores=SC_CORES, num_subcores=SC_SUBCORES)


def _sc_sequence_index():
    return lax.axis_index("s") * SC_CORES + lax.axis_index("c")


def _sb_sample_scores(kt_slabs, idx, q_lanes, *, heads):
    nb, ng, gk = idx.shape
    _, hd, page = kt_slabs.shape
    groups_per_head = ng // heads
    nv = page // SC_LANES

    @functools.partial(
        pl.kernel, mesh=_sc_mesh(),
        out_type=jax.ShapeDtypeStruct((nb, ng * gk, page), F32),
        scratch_types=[pltpu.VMEM((ng, gk), jnp.int32), pltpu.VMEM(q_lanes.shape[1:], F32),
                       pltpu.VMEM((2, gk, hd, page), F32), pltpu.VMEM((gk, page), F32),
                       pltpu.SemaphoreType.DMA((2,))])
    def kern(kt_hbm, idx_hbm, q_hbm, z_hbm, idx_v, q_v, buf, zrow, sems):
        b = _sc_sequence_index()
        pltpu.sync_copy(idx_hbm.at[b], idx_v)
        pltpu.sync_copy(q_hbm.at[b], q_v)

        def gather(g, slot):
            return pltpu.make_async_copy(kt_hbm.at[idx_v.at[g]], buf.at[slot], sems.at[slot])

        gather(0, 0).start()

        @pl.loop(0, ng, step=2)
        def _(g0):
            for slot in range(2):
                g = g0 + slot

                @pl.when(g + 1 < ng)
                def _():
                    gather(g + 1, 1 - slot).start()

                gather(g, slot).wait()
                h = g // groups_per_head
                for i in range(gk):
                    for v in range(nv):
                        zrow[i, pl.ds(v * SC_LANES, SC_LANES)] = jnp.zeros((SC_LANES,), F32)

                    @pl.loop(0, hd, step=SC_DIM_CHUNK)
                    def _(d0):
                        accs = [None] * nv
                        for dd in range(SC_DIM_CHUNK):
                            qv = q_v[h * (hd // SC_DIM_CHUNK) + d0 // SC_DIM_CHUNK, pl.ds(dd * SC_LANES, SC_LANES)]
                            for v in range(nv):
                                t = qv * buf[slot, i, d0 + dd, pl.ds(v * SC_LANES, SC_LANES)]
                                accs[v] = t if accs[v] is None else accs[v] + t
                        for v in range(nv):
                            plsc.addupdate(zrow.at[i, pl.ds(v * SC_LANES, SC_LANES)], accs[v])
                pltpu.sync_copy(zrow, z_hbm.at[b, pl.ds(g * gk, gk)])

    return kern(kt_slabs, idx, q_lanes)


def _sb_sample_values(vt_slabs, idx, w, *, heads):
    nb, ng, gk = idx.shape
    _, hd, page = vt_slabs.shape
    groups_per_head = ng // heads
    nv = page // SC_LANES
    acc_rows = hd // SC_DIM_CHUNK

    @functools.partial(
        pl.kernel, mesh=_sc_mesh(),
        out_type=jax.ShapeDtypeStruct((nb, heads, acc_rows, page), F32),
        scratch_types=[pltpu.VMEM((ng, gk), jnp.int32), pltpu.VMEM((2, gk, hd, page), F32),
                       pltpu.VMEM((2, gk, page), F32), pltpu.VMEM((acc_rows, page), F32),
                       pltpu.SemaphoreType.DMA((2,)), pltpu.SemaphoreType.DMA((2,))])
    def kern(vt_hbm, idx_hbm, w_hbm, o_hbm, idx_v, buf, wbuf, acc, sems, wsems):
        b = _sc_sequence_index()
        pltpu.sync_copy(idx_hbm.at[b], idx_v)

        def copies(g, slot):
            return (pltpu.make_async_copy(vt_hbm.at[idx_v.at[g]], buf.at[slot], sems.at[slot]),
                    pltpu.make_async_copy(w_hbm.at[b, pl.ds(g * gk, gk)], wbuf.at[slot], wsems.at[slot]))

        for c in copies(0, 0):
            c.start()

        @pl.loop(0, heads)
        def _(h):
            for r in range(acc_rows):
                for v in range(nv):
                    acc[r, pl.ds(v * SC_LANES, SC_LANES)] = jnp.zeros((SC_LANES,), F32)

            @pl.loop(0, groups_per_head, step=2)
            def _(gg):
                for slot in range(2):
                    g = h * groups_per_head + gg + slot

                    @pl.when(g + 1 < ng)
                    def _():
                        for c in copies(g + 1, 1 - slot):
                            c.start()

                    for c in copies(g, slot):
                        c.wait()
                    for i in range(gk):
                        @pl.loop(0, hd, step=SC_DIM_CHUNK)
                        def _(d0):
                            wv = [wbuf[slot, i, pl.ds(v * SC_LANES, SC_LANES)] for v in range(nv)]
                            for dd in range(SC_DIM_CHUNK):
                                t = wv[0] * buf[slot, i, d0 + dd, pl.ds(0, SC_LANES)]
                                for v in range(1, nv):
                                    t = t + wv[v] * buf[slot, i, d0 + dd, pl.ds(v * SC_LANES, SC_LANES)]
                                plsc.addupdate(acc.at[d0 // SC_DIM_CHUNK, pl.ds(dd * SC_LANES, SC_LANES)], t)

            pltpu.sync_copy(acc, o_hbm.at[b, h])

    return kern(vt_slabs, idx, w)


def _sb_sample_weights_kernel(z_ref, q_ref, knew_ref, vnew_ref, bias_ref, _order_ref, w_ref, onew_ref, carry, *,
                              new_key_visible):
    bias = bias_ref[...]

    @pl.when(pl.program_id(0) == 0)
    def _():
        z_new = jnp.sum(q_ref[...] * knew_ref[...], axis=-1, keepdims=True) + bias
        ls_new, lk_new = _log_sigmoid_pair(z_new)
        vis = jnp.full(z_new.shape, new_key_visible)
        carry[...] = jnp.where(vis, lk_new, 0.0)
        onew_ref[...] = jnp.where(vis, jnp.exp(ls_new), 0.0) * vnew_ref[...]

    ls, lk = _log_sigmoid_pair(z_ref[...] + bias)
    suffix = _suffix_matrix(MXU_DIM, BF16)
    run = carry[...]
    after = []
    for c in reversed(range(z_ref.shape[1] // MXU_DIM)):
        lkc = lk[:, c * MXU_DIM:(c + 1) * MXU_DIM]
        after.append(_split_dot(lkc, suffix) + run)
        run = run + jnp.sum(lkc, axis=-1, keepdims=True)
    carry[...] = run
    w_ref[...] = jnp.exp(ls + jnp.concatenate(after[::-1], axis=1))


def _sb_sample_weights(z, q, k_new, v_new, bias_rows, run_after, *, new_key_visible):
    rows, keys = z.shape
    hd = q.shape[1]
    chunk = SB_SAMPLE_KEY_CHUNK
    nsteps = keys // chunk
    zspec = pl.BlockSpec((rows, chunk), lambda c: (0, nsteps - 1 - c))
    small = lambda w: pl.BlockSpec((rows, w), lambda c: (0, 0))
    return pl.pallas_call(
        functools.partial(_sb_sample_weights_kernel, new_key_visible=new_key_visible),
        grid=(nsteps,),
        in_specs=[zspec, small(hd), small(hd), small(hd), small(1),
                  pl.BlockSpec(run_after.shape, lambda c: (0,) * run_after.ndim)],
        out_specs=[zspec, small(hd)],
        out_shape=[jax.ShapeDtypeStruct((rows, keys), F32), jax.ShapeDtypeStruct((rows, hd), F32)],
        scratch_shapes=[pltpu.VMEM((rows, 1), F32)],
        compiler_params=_params("arbitrary"),
        name="sb_sample_weights",
    )(z, q, k_new, v_new, bias_rows, run_after)


def _sb_sample_finish_kernel(parts_ref, onew_ref, o_ref):
    lanes = parts_ref.shape[1]
    j = lax.broadcasted_iota(jnp.int32, (lanes, lanes), 0)
    c = lax.broadcasted_iota(jnp.int32, (lanes, lanes), 1)
    group_sum = jnp.where(j // SC_LANES == c, 1.0, 0.0).astype(BF16)
    o_ref[...] = _split_dot(parts_ref[...], group_sum)[:, :o_ref.shape[1]] + onew_ref[...]


def _sb_sample_finish(parts, o_new):
    rows, lanes = parts.shape
    cols = o_new.shape[1]
    return pl.pallas_call(
        _sb_sample_finish_kernel,
        grid=(1,),
        in_specs=[pl.BlockSpec((rows, lanes), lambda i: (0, 0)), pl.BlockSpec((rows, cols), lambda i: (0, 0))],
        out_specs=pl.BlockSpec((rows, cols), lambda i: (0, 0)),
        out_shape=jax.ShapeDtypeStruct((rows, cols), F32),
        compiler_params=_params("arbitrary"),
        name="sb_sample_finish",
    )(parts, o_new)


def _sb_sample(page_table, q, k_new, v_new, bias, cache_k, cache_v, weights_after, *, new_key_visible):
    nb, n_pages = page_table.shape
    n_pool, page, heads, hd = cache_k.shape
    assert nb == SC_CORES * SC_SUBCORES and page % SC_LANES == 0 and hd % SC_DIM_CHUNK == 0
    assert page == SC_DIM_CHUNK * SC_LANES and n_pages % (2 * SC_GATHER_SLABS) == 0
    slabs = lambda c: c.transpose(0, 2, 3, 1).reshape(n_pool * heads, hd, page)
    idx = (page_table[:, None, :] * heads + jnp.arange(heads, dtype=jnp.int32)[None, :, None]).reshape(
        nb, heads * n_pages // SC_GATHER_SLABS, SC_GATHER_SLABS)
    q_lanes = jnp.repeat(q, SC_LANES, axis=-1).reshape(nb, heads * hd // SC_DIM_CHUNK, page)
    z = _sb_sample_scores(slabs(cache_k), idx, q_lanes, heads=heads)
    per_head = lambda a: a.reshape(nb * heads, hd)
    w, o_new = _sb_sample_weights(z.reshape(nb * heads, n_pages * page), per_head(q), per_head(k_new), per_head(v_new),
                                  jnp.tile(bias, nb).reshape(nb * heads, 1), weights_after,
                                  new_key_visible=new_key_visible)
    parts = _sb_sample_values(slabs(cache_v), idx, w.reshape(nb, heads * n_pages, page), heads=heads)
    o = _sb_sample_finish(parts.reshape(nb * heads * hd // SC_DIM_CHUNK, page),
                          o_new.reshape(nb * heads * hd // SC_DIM_CHUNK, SC_DIM_CHUNK))
    return o.reshape(nb, heads * hd)


def _ssm_prep_kernel(are_ref, aim_ref, ldt_ref, bre_ref, bim_ref, bcat_ref, abr_ref, abi_ref):
    dt = jnp.exp(ldt_ref[...])
    lam_re = jnp.minimum(are_ref[...], A_RE_CLIP)
    lam_im = aim_ref[...]
    mag = jnp.exp(lam_re * dt)
    ab_re = mag * jnp.cos(lam_im * dt)
    ab_im = mag * jnp.sin(lam_im * dt)
    den = lam_re * lam_re + lam_im * lam_im
    nr = ab_re - 1.0
    f_re = (nr * lam_re + ab_im * lam_im) / den
    f_im = (ab_im * lam_re - nr * lam_im) / den
    abr_ref[...] = ab_re
    abi_ref[...] = ab_im
    br, bi = bre_ref[...], bim_ref[...]
    gp = br.shape[1]
    bcat_ref[:, :gp] = (f_re * br - f_im * bi).astype(BF16)
    bcat_ref[:, gp:] = (f_re * bi + f_im * br).astype(BF16)


def _ssm_prep(a_re, a_im, log_dt_full, b_re_bd, b_im_bd):
    gc, gp = b_re_bd.shape
    vec = _const_spec((1, gp))
    return pl.pallas_call(
        _ssm_prep_kernel,
        grid=(1,),
        in_specs=[vec, vec, vec, _const_spec((gc, gp)), _const_spec((gc, gp))],
        out_specs=[pl.BlockSpec((gc, 2 * gp), lambda i: (0, 0)),
                   pl.BlockSpec((1, gp), lambda i: (0, 0)), pl.BlockSpec((1, gp), lambda i: (0, 0))],
        out_shape=[jax.ShapeDtypeStruct((gc, 2 * gp), BF16),
                   jax.ShapeDtypeStruct((1, gp), F32), jax.ShapeDtypeStruct((1, gp), F32)],
        compiler_params=_params("arbitrary"),
        name="ssm_prep",
    )(a_re, a_im, log_dt_full, b_re_bd, b_im_bd)


def _gelu_tanh(y):
    return 0.5 * y * (1.0 + jnp.tanh(math.sqrt(2.0 / math.pi) * (y + 0.044715 * (y * y * y))))


def _ssm_kernel(*refs, steps, nb, has_init):
    (u_ref, bcat_ref, cre_ref, cim_ref, abr_ref, abi_ref, d_ref, wglu_ref, bglu_ref), refs = refs[:9], refs[9:]
    if has_init:
        (s0re_ref, s0im_ref), refs = refs[:2], refs[2:]
    o_ref, sfre_ref, sfim_ref, sre, sim, st_re, st_im = refs
    c = pl.program_id(0)
    gp = sre.shape[1]

    @pl.when(c == 0)
    def _():
        if has_init:
            st_re[...] = s0re_ref[...]
            st_im[...] = s0im_ref[...]
        else:
            st_re[...] = jnp.zeros_like(st_re)
            st_im[...] = jnp.zeros_like(st_im)

    u = u_ref[...]
    ub = u.astype(BF16)
    slabs = u.shape[1] // LANES
    sw = gp // slabs
    for m in range(slabs):
        um = ub[:, m * LANES:(m + 1) * LANES]
        sre[:, m * sw:(m + 1) * sw] = _dot(um, bcat_ref[m * LANES:(m + 1) * LANES, m * sw:(m + 1) * sw])
        sim[:, m * sw:(m + 1) * sw] = _dot(um, bcat_ref[m * LANES:(m + 1) * LANES, gp + m * sw:gp + (m + 1) * sw])
    a_re = jnp.broadcast_to(abr_ref[...], (nb, gp))
    a_im = jnp.broadcast_to(abi_ref[...], (nb, gp))

    def step(t, state):
        s_re, s_im = state
        rows = pl.ds(pl.multiple_of(t * nb, nb), nb)
        n_re = a_re * s_re - a_im * s_im + sre[rows, :]
        n_im = a_re * s_im + a_im * s_re + sim[rows, :]
        sre[rows, :] = n_re
        sim[rows, :] = n_im
        return n_re, n_im

    s_re, s_im = lax.fori_loop(0, steps, step, (st_re[...], st_im[...]), unroll=min(steps, SSM_SCAN_UNROLL))
    st_re[...] = s_re
    st_im[...] = s_im

    y = jnp.concatenate(
        [_dot(sre[:, m * sw:(m + 1) * sw].astype(BF16), cre_ref[m * sw:(m + 1) * sw, m * LANES:(m + 1) * LANES])
         - _dot(sim[:, m * sw:(m + 1) * sw].astype(BF16), cim_ref[m * sw:(m + 1) * sw, m * LANES:(m + 1) * LANES])
         for m in range(slabs)], axis=1)
    y = y + d_ref[...] * u
    g = _gelu_tanh(y)
    o_ref[...] = g * jax.nn.sigmoid(_dot(g.astype(BF16), wglu_ref[...]) + bglu_ref[...])

    @pl.when(c == pl.num_programs(0) - 1)
    def _():
        sfre_ref[...] = s_re
        sfim_ref[...] = s_im


def _ssm(u_tb, bcat, cre_bd, cim_bd, ab_re, ab_im, d_skip, w_glu, b_glu, s0, *, steps, nb):
    rows, width = u_tb.shape
    gp = ab_re.shape[1]
    chunk = steps * nb
    has_init = s0 is not None
    tile = pl.BlockSpec((chunk, width), lambda c: (c, 0))
    state = _const_spec((nb, gp))
    in_specs = [tile, _const_spec(bcat.shape), _const_spec(cre_bd.shape), _const_spec(cim_bd.shape),
                _const_spec((1, gp)), _const_spec((1, gp)), _const_spec((1, width)),
                _const_spec(w_glu.shape), _const_spec((1, width))]
    args = [u_tb, bcat, cre_bd, cim_bd, ab_re, ab_im, d_skip, w_glu, b_glu]
    if has_init:
        in_specs += [state, state]
        args += list(s0)
    return pl.pallas_call(
        functools.partial(_ssm_kernel, steps=steps, nb=nb, has_init=has_init),
        grid=(rows // chunk,),
        in_specs=in_specs,
        out_specs=[tile, pl.BlockSpec((nb, gp), lambda c: (0, 0)), pl.BlockSpec((nb, gp), lambda c: (0, 0))],
        out_shape=[jax.ShapeDtypeStruct((rows, width), F32),
                   jax.ShapeDtypeStruct((nb, gp), F32), jax.ShapeDtypeStruct((nb, gp), F32)],
        scratch_shapes=[pltpu.VMEM((chunk, gp), F32), pltpu.VMEM((chunk, gp), F32),
                        pltpu.VMEM((nb, gp), F32), pltpu.VMEM((nb, gp), F32)],
        compiler_params=_params("arbitrary"),
        name="ssm",
    )(*args)


def _mem_kv_kernel(m_ref, g_ref, wk_ref, wv_ref, k_ref, v_ref, kb_ref, vb_ref):
    h = _rms(m_ref[...], g_ref[...]).astype(BF16)
    k = _dot(h, wk_ref[...])
    v = _dot(h, wv_ref[...])
    k_ref[...] = k
    v_ref[...] = v
    kb_ref[...] = k.astype(BF16)
    vb_ref[...] = v.astype(BF16)


def _mem_kv(mem, g, wk, wv, *, tm):
    n, d = mem.shape
    row = pl.BlockSpec((tm, d), lambda i: (i, 0))
    return pl.pallas_call(
        _mem_kv_kernel,
        grid=(n // tm,),
        in_specs=[row, _const_spec((1, d)), _const_spec(wk.shape), _const_spec(wv.shape)],
        out_specs=[row] * 4,
        out_shape=[jax.ShapeDtypeStruct((n, d), F32)] * 2 + [jax.ShapeDtypeStruct((n, d), BF16)] * 2,
        compiler_params=_params("parallel"),
        name="mem_kv",
    )(mem, g, wk, wv)


def _mix_out_kernel(x1_ref, osb_ref, ossm_ref, gsb_ref, gssm_ref, wout_ref, gx_ref, wxq_ref, x2_ref, qx_ref, *, xscale):
    mixed = jnp.concatenate([_rms(osb_ref[...], gsb_ref[...]), _rms(ossm_ref[...], gssm_ref[...])], axis=-1)
    x2 = x1_ref[...] + _dot(mixed.astype(BF16), wout_ref[...])
    x2_ref[...] = x2
    hx = _rms(x2, gx_ref[...]).astype(BF16)
    qx_ref[...] = (_dot(hx, wxq_ref[...]) * xscale).astype(BF16)


def _mix_out(x1, o_sb, o_ssm, g_sb, g_ssm, w_out, g_x, w_xq, *, tm, ssm_index, xscale):
    n, d = x1.shape
    sbw = o_sb.shape[1]
    ssw = g_ssm.shape[1]
    row = lambda w: pl.BlockSpec((tm, w), lambda i: (i, 0))
    return pl.pallas_call(
        functools.partial(_mix_out_kernel, xscale=xscale),
        grid=(n // tm,),
        in_specs=[row(d), row(sbw), pl.BlockSpec((tm, ssw), ssm_index), _const_spec((1, sbw)), _const_spec((1, ssw)),
                  _const_spec(w_out.shape), _const_spec((1, d)), _const_spec(w_xq.shape)],
        out_specs=[row(d), row(d)],
        out_shape=[jax.ShapeDtypeStruct((n, d), F32), jax.ShapeDtypeStruct((n, d), BF16)],
        compiler_params=_params("parallel"),
        name="mix_out",
    )(x1, o_sb, o_ssm, g_sb, g_ssm, w_out, g_x, w_xq)


def _xattn_kernel(q_ref, k_ref, v_ref, o_ref, *, rows):
    q = q_ref[0]
    if q.shape[0] != rows:
        q = jnp.broadcast_to(q, (rows, q.shape[1]))
    hd = q.shape[1] // X_HEADS

    def mem_head(ref, h):
        return ref[0, :, h * hd:(h + 1) * hd].astype(BF16)

    outs = []
    for h in range(X_HEADS):
        s = _dot_nt(q[:, h * hd:(h + 1) * hd], mem_head(k_ref, h))
        e = jnp.exp(s - jnp.max(s, axis=-1, keepdims=True))
        p = e / jnp.sum(e, axis=-1, keepdims=True)
        outs.append(_dot(p.astype(BF16), mem_head(v_ref, h)))
    o = jnp.concatenate(outs, axis=-1).astype(o_ref.dtype)
    o_ref[0] = o[:o_ref.shape[1]]


def _xattn(q, mem_k, mem_v, *, tq):
    b, l, d = q.shape
    m = mem_k.shape[1]
    rows = max(tq, 16)
    qspec = pl.BlockSpec((1, tq, d), lambda bi, i: (bi, i, 0))
    mspec = pl.BlockSpec((1, m, d), lambda bi, i: (bi, 0, 0))
    return pl.pallas_call(
        functools.partial(_xattn_kernel, rows=rows),
        grid=(b, l // tq),
        in_specs=[qspec, mspec, mspec],
        out_specs=qspec,
        out_shape=jax.ShapeDtypeStruct((b, l, d), BF16),
        compiler_params=_params("parallel", "arbitrary"),
        name="xattn",
    )(q, mem_k, mem_v)


def _out_kernel(x2_ref, ox_ref, wxo_ref, g2_ref, wg_ref, wu_ref, wd_ref, gfin_ref, y_ref):
    x3 = x2_ref[...] + _dot(ox_ref[...], wxo_ref[...])
    x4 = _ffn_half(x3, g2_ref, wg_ref, wu_ref, wd_ref)
    y_ref[...] = _rms(x4, gfin_ref[...])


def _out(x2, ox, w_xo, g2, wg, wu, wd, g_fin, *, tm):
    n, d = x2.shape
    dff = wg.shape[1]
    row = pl.BlockSpec((tm, d), lambda i: (i, 0))
    return pl.pallas_call(
        _out_kernel,
        grid=(n // tm,),
        in_specs=[row, row, _const_spec(w_xo.shape), _const_spec((1, d)), _const_spec((d, dff)),
                  _const_spec((d, dff)), _const_spec((dff, d)), _const_spec((1, d))],
        out_specs=row,
        out_shape=jax.ShapeDtypeStruct((n, d), F32),
        compiler_params=_params("parallel"),
        name="ffn2_out",
    )(x2, ox, w_xo, g2, wg, wu, wd, g_fin)


def _block_diag(x):
    g, r, c = x.shape
    row_block = jnp.arange(g * r)[:, None] // r
    col_block = jnp.arange(g * c)[None, :] // c
    return jnp.where(row_block == col_block, jnp.tile(x.reshape(g * r, c), (1, g)), 0)


def kernel(x_prompt, x_sample, cache_k, cache_v, state_ssm_re, state_ssm_im, cache_mem_k, cache_mem_v, page_table, mem_prompt, g_ffn1, w_ffn1_gate, w_ffn1_up, w_ffn1_down, g_mix, w_in, sb_bias, ssm_a_re, ssm_a_im, ssm_log_dt, ssm_b_re, ssm_b_im, ssm_c_re, ssm_c_im, ssm_d, w_glu, b_glu, g_sb_out, g_ssm_out, w_out, g_xattn, g_mem, w_xq, w_xk, w_xv, w_xo, g_ffn2, w_ffn2_gate, w_ffn2_up, w_ffn2_down, g_final):
    depth = w_in.shape[0]
    assert depth == 1, "single-layer step"
    bsz, seq, d = x_prompt.shape
    dec_b, dec_seq, _ = x_sample.shape
    assert dec_seq == 1
    n_pool, page, sb_heads, hd = cache_k.shape[1:]
    assert hd == SB_HEAD_DIM
    sbw = sb_heads * hd
    groups, states = ssm_a_re.shape[1:]
    gp = groups * states
    ssw = groups * SSM_GROUP
    n_mem = mem_prompt.shape[1]
    past_len = page_table.shape[1] * page
    xscale = (d // X_HEADS) ** -0.5

    bf = lambda w: w[0].astype(BF16)
    vec = lambda g: g.reshape(1, -1)
    ffn1 = (vec(g_ffn1[0]), bf(w_ffn1_gate), bf(w_ffn1_up), bf(w_ffn1_down))
    ffn2 = (vec(g_ffn2[0]), bf(w_ffn2_gate), bf(w_ffn2_up), bf(w_ffn2_down))
    w_in_b, w_out_b, w_xq_b, w_xo_b = bf(w_in), bf(w_out), bf(w_xq), bf(w_xo)

    bcat, ab_re, ab_im = _ssm_prep(
        ssm_a_re[0].reshape(1, gp), ssm_a_im[0].reshape(1, gp),
        jnp.repeat(ssm_log_dt[0], states).reshape(1, gp),
        _block_diag(ssm_b_re[0].transpose(0, 2, 1)), _block_diag(ssm_b_im[0].transpose(0, 2, 1)))
    cre_bd = _block_diag(ssm_c_re[0].transpose(0, 2, 1)).astype(BF16)
    cim_bd = _block_diag(ssm_c_im[0].transpose(0, 2, 1)).astype(BF16)
    ssm_w = (bcat, cre_bd, cim_bd, ab_re, ab_im, vec(ssm_d[0]), bf(w_glu), vec(b_glu[0]))

    row0 = lambda i: (0, 0)
    x1s, k_s, v_s, qbs, _, _, u_s = _in_proj(
        x_sample.reshape(dec_b, d), *ffn1, vec(g_mix[0]), w_in_b,
        tm=dec_b, sb_width=sbw, u_shape=(dec_b, ssw), u_index=row0)

    tm = PROMPT_TOKEN_TILE
    tiles = seq // tm
    tb_index = lambda i: (i % tiles, i // tiles)
    n = bsz * seq
    x1, k_p, v_p, qb, kb, vb, u_tb = _in_proj(
        x_prompt.reshape(n, d), *ffn1, vec(g_mix[0]), w_in_b,
        tm=tm, sb_width=sbw, u_shape=(seq, bsz * ssw), u_index=tb_index, kv_time_tiles=tiles)
    o_sb = _sb_prompt(sb_bias[0], qb.reshape(bsz, seq, sbw), kb.reshape(bsz, seq, sbw), vb.reshape(bsz, seq, sbw),
                      blk=SB_BLOCK)
    o_ssm, sre_p, sim_p = _ssm(u_tb.reshape(seq * bsz, ssw), *ssm_w, None, steps=SSM_CHUNK_STEPS, nb=bsz)
    o_sb_s = _sb_sample(page_table, qbs.astype(F32), k_s, v_s, sb_bias[0], cache_k[0], cache_v[0], sre_p,
                        new_key_visible=bool(past_len < past_len))
    mk_p, mv_p, mkb, mvb = _mem_kv(mem_prompt.reshape(bsz * n_mem, d), vec(g_mem[0]), bf(w_xk), bf(w_xv), tm=tm)
    x2, qx = _mix_out(x1, o_sb.reshape(n, sbw), o_ssm.reshape(seq, bsz * ssw), vec(g_sb_out[0]), vec(g_ssm_out[0]),
                      w_out_b, vec(g_xattn[0]), w_xq_b, tm=tm, ssm_index=tb_index, xscale=xscale)
    ox = _xattn(qx.reshape(bsz, seq, d), mkb.reshape(bsz, n_mem, d), mvb.reshape(bsz, n_mem, d), tq=XATTN_Q_TILE)
    y_p = _out(x2, ox.reshape(n, d), w_xo_b, *ffn2, vec(g_final), tm=tm)

    o_ssm_s, sre_s, sim_s = _ssm(u_s, *ssm_w, (state_ssm_re[0].reshape(dec_b, gp), state_ssm_im[0].reshape(dec_b, gp)),
                                 steps=1, nb=dec_b)
    x2s, qxs = _mix_out(x1s, o_sb_s, o_ssm_s, vec(g_sb_out[0]), vec(g_ssm_out[0]),
                        w_out_b, vec(g_xattn[0]), w_xq_b, tm=dec_b, ssm_index=row0, xscale=xscale)
    mem_s = lambda c: c[0].reshape(dec_b, n_mem, d)
    oxs = _xattn(qxs.reshape(dec_b, 1, d), mem_s(cache_mem_k), mem_s(cache_mem_v), tq=1)
    y_s = _out(x2s, oxs.reshape(dec_b, d), w_xo_b, *ffn2, vec(g_final), tm=dec_b)

    kv = lambda a, b_, l_: a.reshape(1, b_, l_, sb_heads, hd)
    kv_t = lambda a: a.reshape(1, bsz, sb_heads, hd, seq).transpose(0, 1, 4, 2, 3)
    st = lambda a, b_: a.reshape(1, b_, groups, states)
    mem = lambda a: a.reshape(1, bsz, n_mem, X_HEADS, d // X_HEADS)
    return (y_p.reshape(bsz, seq, d), y_s.reshape(dec_b, 1, d),
            kv_t(k_p), kv_t(v_p), kv(k_s, dec_b, 1), kv(v_s, dec_b, 1),
            st(sre_p, bsz), st(sim_p, bsz), st(sre_s, dec_b), st(sim_s, dec_b),
            mem(mk_p), mem(mv_p))
```

```python
import functools
import math

import jax
import jax.numpy as jnp
from jax import lax
from jax.experimental import pallas as pl
from jax.experimental.pallas import tpu as pltpu
from jax.experimental.pallas import tpu_sc as plsc

F32 = jnp.float32
BF16 = jnp.bfloat16

EPS = 1e-6
A_RE_CLIP = -1e-4
SB_HEAD_DIM = 64
SSM_GROUP = 16
X_HEADS = 4

LANES = 128
MXU_DIM = 256
VMEM_LIMIT_BYTES = 56 * 1024 * 1024
SC_CORES = 2
SC_SUBCORES = 16
SC_LANES = 16
SC_GATHER_SLABS = 4
SC_DIM_CHUNK = 8
MEM_COPY_ROWS = 32

PROMPT_TOKEN_TILE = 256
SB_BLOCK = 256
SSM_CHUNK_STEPS = 64
SSM_SCAN_UNROLL = 4
SB_SAMPLE_KEY_CHUNK = 1024
XATTN_Q_TILE = 512


def _params(*semantics):
    return pltpu.CompilerParams(dimension_semantics=semantics, vmem_limit_bytes=VMEM_LIMIT_BYTES)


def _const_spec(shape):
    return pl.BlockSpec(shape, lambda *_: (0,) * len(shape), pipeline_mode=pl.Buffered(1))


def _dot(a, b):
    return jnp.dot(a, b, preferred_element_type=F32)


def _dot_nt(a, b):
    return lax.dot_general(a, b, (((1,), (1,)), ((), ())), preferred_element_type=F32)


def _rms(x, g):
    return x * lax.rsqrt(jnp.mean(x * x, axis=-1, keepdims=True) + EPS) * g


def _ffn_half(x, g_ref, wg_ref, wu_ref, wd_ref):
    h = _rms(x, g_ref[...]).astype(BF16)
    gate = _dot(h, wg_ref[...])
    up = _dot(h, wu_ref[...])
    act = (gate * jax.nn.sigmoid(gate) * up).astype(BF16)
    return x + 0.5 * _dot(act, wd_ref[...])


def _log_sigmoid_pair(z):
    ls = jnp.minimum(z, 0.0) - jnp.log(1.0 + jnp.exp(-jnp.abs(z)))
    return ls, ls - z


def _suffix_matrix(n, dtype):
    j = lax.broadcasted_iota(jnp.int32, (n, n), 0)
    s = lax.broadcasted_iota(jnp.int32, (n, n), 1)
    return jnp.where(j > s, 1.0, 0.0).astype(dtype)


def _split_dot(x, m):
    hi = x.astype(BF16)
    lo = (x - hi.astype(F32)).astype(BF16)
    return _dot(hi, m) + _dot(lo, m)


def _in_proj_kernel(x_ref, g1_ref, wg_ref, wu_ref, wd_ref, gmix_ref, win_ref,
                    x1_ref, k_ref, v_ref, qb_ref, kb_ref, vb_ref, u_ref, *, sb_width, kv_transposed):
    x1 = _ffn_half(x_ref[...], g1_ref, wg_ref, wu_ref, wd_ref)
    x1_ref[...] = x1
    h = _rms(x1, gmix_ref[...]).astype(BF16)
    proj = _dot(h, win_ref[...])
    q = proj[:, :sb_width]
    k = proj[:, sb_width:2 * sb_width]
    v = proj[:, 2 * sb_width:3 * sb_width]
    if kv_transposed:
        k_ref[0] = k.T
        v_ref[0] = v.T
    else:
        k_ref[...] = k
        v_ref[...] = v
    qb_ref[...] = (q * SB_HEAD_DIM ** -0.5).astype(BF16)
    kb_ref[...] = k.astype(BF16)
    vb_ref[...] = v.astype(BF16)
    u_ref[...] = proj[:, 3 * sb_width:]


def _in_proj(x, g1, wg, wu, wd, gmix, win, *, tm, sb_width, u_shape, u_index, kv_time_tiles=None):
    n, d = x.shape
    dff = wg.shape[1]
    ssm_width = win.shape[1] - 3 * sb_width
    row = lambda w: pl.BlockSpec((tm, w), lambda i: (i, 0))
    if kv_time_tiles is None:
        kv_spec = row(sb_width)
        kv_shape = jax.ShapeDtypeStruct((n, sb_width), F32)
    else:
        t = kv_time_tiles
        kv_spec = pl.BlockSpec((1, sb_width, tm), lambda i: (i // t, 0, i % t))
        kv_shape = jax.ShapeDtypeStruct((n // (t * tm), sb_width, t * tm), F32)
    return pl.pallas_call(
        functools.partial(_in_proj_kernel, sb_width=sb_width, kv_transposed=kv_time_tiles is not None),
        grid=(n // tm,),
        in_specs=[row(d), _const_spec((1, d)), _const_spec((d, dff)), _const_spec((d, dff)),
                  _const_spec((dff, d)), _const_spec((1, d)), _const_spec(win.shape)],
        out_specs=[row(d), kv_spec, kv_spec, row(sb_width), row(sb_width), row(sb_width),
                   pl.BlockSpec((tm, ssm_width), u_index)],
        out_shape=[jax.ShapeDtypeStruct((n, d), F32), kv_shape, kv_shape,
                   jax.ShapeDtypeStruct((n, sb_width), BF16), jax.ShapeDtypeStruct((n, sb_width), BF16),
                   jax.ShapeDtypeStruct((n, sb_width), BF16),
                   jax.ShapeDtypeStruct(u_shape, F32)],
        compiler_params=_params("parallel"),
        name="in_proj",
    )(x, g1, wg, wu, wd, gmix, win)


def _sb_prompt_kernel(bias_ref, q_ref, k_ref, v_ref, o_ref, carry_ref, acc_ref, z_ref, w_ref, *, blk):
    pair = pl.program_id(1)
    nq = q_ref.shape[1] // blk
    lane = lax.broadcasted_iota(jnp.int32, (blk, LANES), 1)
    first = lane < SB_HEAD_DIM
    bias2 = jnp.concatenate(
        [jnp.full((blk, 1), bias_ref[2 * pair]), jnp.full((blk, 1), bias_ref[2 * pair + 1])], axis=0)
    suffix = _suffix_matrix(blk, BF16)

    def block_of(ref, j):
        return ref[0, pl.ds(pl.multiple_of(j * blk, blk), blk), :]

    def scores(qi, j):
        q = block_of(q_ref, qi)
        zero = jnp.zeros_like(q)
        q2 = jnp.concatenate([jnp.where(first, q, zero), jnp.where(first, zero, q)], axis=0)
        return _dot_nt(q2, block_of(k_ref, j))

    def finish_values(j_prev):
        return acc_ref[...] + _dot(w_ref[...], block_of(v_ref, j_prev))

    def write_out(qi, acc):
        o_ref[0, pl.ds(pl.multiple_of(qi * blk, blk), blk), :] = jnp.where(first, acc[:blk], acc[blk:])

    def weights(visible, carry):
        ls, lk = _log_sigmoid_pair(z_ref[...] + bias2)
        if visible is not None:
            lk = jnp.where(visible, lk, 0.0)
        w = jnp.exp(ls + _split_dot(lk, suffix) + carry)
        if visible is not None:
            w = jnp.where(visible, w, 0.0)
        return w.astype(BF16), jnp.sum(lk, axis=-1, keepdims=True)

    qpos = lax.broadcasted_iota(jnp.int32, (2 * blk, blk), 0)
    kpos = lax.broadcasted_iota(jnp.int32, (2 * blk, blk), 1)
    diagonal = kpos < jnp.where(qpos >= blk, qpos - blk, qpos)

    acc_ref[...] = jnp.zeros_like(acc_ref)
    w_ref[...] = jnp.zeros_like(w_ref)
    z_ref[...] = scores(0, 0)

    @pl.loop(0, nq)
    def _(qi):
        w, rowsum = weights(diagonal, 0.0)
        last = qi == nq - 1
        z_ref[...] = scores(jnp.where(qi == 0, jnp.where(last, 0, 1), qi),
                            jnp.where(qi == 0, jnp.where(last, 0, 1), qi - 1))
        write_out(jnp.maximum(qi - 1, 0), finish_values(0))
        acc_ref[...] = jnp.zeros_like(acc_ref)
        w_ref[...] = w
        carry_ref[...] = rowsum

        @pl.loop(0, qi)
        def _(i):
            j = qi - 1 - i
            carry = carry_ref[...]
            w, rowsum = weights(None, carry)
            end = j == 0
            nxt = jnp.minimum(qi + 1, nq - 1)
            z_ref[...] = scores(jnp.where(end, nxt, qi), jnp.where(end, nxt, j - 1))
            acc_ref[...] = finish_values(j + 1)
            w_ref[...] = w
            carry_ref[...] = carry + rowsum

    write_out(nq - 1, finish_values(0))


def _sb_prompt(bias, qb, kb, vb, *, blk):
    b, l, w = qb.shape
    seq_block = pl.BlockSpec((1, l, LANES), lambda bi, p: (bi, 0, p))
    return pl.pallas_call(
        functools.partial(_sb_prompt_kernel, blk=blk),
        grid=(b, w // LANES),
        in_specs=[pl.BlockSpec(memory_space=pltpu.SMEM), seq_block, seq_block, seq_block],
        out_specs=seq_block,
        out_shape=jax.ShapeDtypeStruct((b, l, w), F32),
        scratch_shapes=[pltpu.VMEM((2 * blk, 1), F32), pltpu.VMEM((2 * blk, LANES), F32),
                        pltpu.VMEM((2 * blk, blk), F32), pltpu.VMEM((2 * blk, blk), BF16)],
        compiler_params=_params("parallel", "parallel"),
        name="sb_prompt",
    )(bias, qb, kb, vb)


def _sc_mesh():
    return plsc.VectorSubcoreMesh(core_axis_name="c", subcore_axis_name="s", num_cores=SC_CORES, num_subcores=SC_SUBCORES)


def _sc_sequence_index():
    return lax.axis_index("s") * SC_CORES + lax.axis_index("c")


def _sb_sample_scores(kt_slabs, idx, q_lanes, *, heads):
    nb, ng, gk = idx.shape
    _, hd, page = kt_slabs.shape
    groups_per_head = ng // heads
    nv = page // SC_LANES

    @functools.partial(
        pl.kernel, mesh=_sc_mesh(),
        out_type=jax.ShapeDtypeStruct((nb, ng * gk, page), F32),
        scratch_types=[pltpu.VMEM((ng, gk), jnp.int32), pltpu.VMEM(q_lanes.shape[1:], F32),
                       pltpu.VMEM((2, gk, hd, page), F32), pltpu.VMEM((gk, page), F32),
                       pltpu.SemaphoreType.DMA((2,))])
    def kern(kt_hbm, idx_hbm, q_hbm, z_hbm, idx_v, q_v, buf, zrow, sems):
        b = _sc_sequence_index()
        pltpu.sync_copy(idx_hbm.at[b], idx_v)
        pltpu.sync_copy(q_hbm.at[b], q_v)

        def gather(g, slot):
            return pltpu.make_async_copy(kt_hbm.at[idx_v.at[g]], buf.at[slot], sems.at[slot])

        gather(0, 0).start()

        @pl.loop(0, ng, step=2)
        def _(g0):
            for slot in range(2):
                g = g0 + slot

                @pl.when(g + 1 < ng)
                def _():
                    gather(g + 1, 1 - slot).start()

                gather(g, slot).wait()
                h = g // groups_per_head
                for i in range(gk):
                    for v in range(nv):
                        zrow[i, pl.ds(v * SC_LANES, SC_LANES)] = jnp.zeros((SC_LANES,), F32)

                    @pl.loop(0, hd, step=SC_DIM_CHUNK)
                    def _(d0):
                        accs = [None] * nv
                        for dd in range(SC_DIM_CHUNK):
                            qv = q_v[h * (hd // SC_DIM_CHUNK) + d0 // SC_DIM_CHUNK, pl.ds(dd * SC_LANES, SC_LANES)]
                            for v in range(nv):
                                t = qv * buf[slot, i, d0 + dd, pl.ds(v * SC_LANES, SC_LANES)]
                                accs[v] = t if accs[v] is None else accs[v] + t
                        for v in range(nv):
                            plsc.addupdate(zrow.at[i, pl.ds(v * SC_LANES, SC_LANES)], accs[v])
                pltpu.sync_copy(zrow, z_hbm.at[b, pl.ds(g * gk, gk)])

    return kern(kt_slabs, idx, q_lanes)


def _sb_sample_values(vt_slabs, idx, w, *, heads):
    nb, ng, gk = idx.shape
    _, hd, page = vt_slabs.shape
    groups_per_head = ng // heads
    nv = page // SC_LANES
    acc_rows = hd // SC_DIM_CHUNK

    @functools.partial(
        pl.kernel, mesh=_sc_mesh(),
        out_type=jax.ShapeDtypeStruct((nb, heads, acc_rows, page), F32),
        scratch_types=[pltpu.VMEM((ng, gk), jnp.int32), pltpu.VMEM((2, gk, hd, page), F32),
                       pltpu.VMEM((2, gk, page), F32), pltpu.VMEM((acc_rows, page), F32),
                       pltpu.SemaphoreType.DMA((2,)), pltpu.SemaphoreType.DMA((2,))])
    def kern(vt_hbm, idx_hbm, w_hbm, o_hbm, idx_v, buf, wbuf, acc, sems, wsems):
        b = _sc_sequence_index()
        pltpu.sync_copy(idx_hbm.at[b], idx_v)

        def copies(g, slot):
            return (pltpu.make_async_copy(vt_hbm.at[idx_v.at[g]], buf.at[slot], sems.at[slot]),
                    pltpu.make_async_copy(w_hbm.at[b, pl.ds(g * gk, gk)], wbuf.at[slot], wsems.at[slot]))

        for c in copies(0, 0):
            c.start()

        @pl.loop(0, heads)
        def _(h):
            for r in range(acc_rows):
                for v in range(nv):
                    acc[r, pl.ds(v * SC_LANES, SC_LANES)] = jnp.zeros((SC_LANES,), F32)

            @pl.loop(0, groups_per_head, step=2)
            def _(gg):
                for slot in range(2):
                    g = h * groups_per_head + gg + slot

                    @pl.when(g + 1 < ng)
                    def _():
                        for c in copies(g + 1, 1 - slot):
                            c.start()

                    for c in copies(g, slot):
                        c.wait()
                    for i in range(gk):
                        @pl.loop(0, hd, step=SC_DIM_CHUNK)
                        def _(d0):
                            wv = [wbuf[slot, i, pl.ds(v * SC_LANES, SC_LANES)] for v in range(nv)]
                            for dd in range(SC_DIM_CHUNK):
                                t = wv[0] * buf[slot, i, d0 + dd, pl.ds(0, SC_LANES)]
                                for v in range(1, nv):
                                    t = t + wv[v] * buf[slot, i, d0 + dd, pl.ds(v * SC_LANES, SC_LANES)]
                                plsc.addupdate(acc.at[d0 // SC_DIM_CHUNK, pl.ds(dd * SC_LANES, SC_LANES)], t)

            pltpu.sync_copy(acc, o_hbm.at[b, h])

    return kern(vt_slabs, idx, w)


def _sb_sample_weights_kernel(z_ref, q_ref, knew_ref, vnew_ref, bias_ref, _order_ref, w_ref, onew_ref, carry, *,
                              new_key_visible):
    bias = bias_ref[...]

    @pl.when(pl.program_id(0) == 0)
    def _():
        z_new = jnp.sum(q_ref[...] * knew_ref[...], axis=-1, keepdims=True) + bias
        ls_new, lk_new = _log_sigmoid_pair(z_new)
        vis = jnp.full(z_new.shape, new_key_visible)
        carry[...] = jnp.where(vis, lk_new, 0.0)
        onew_ref[...] = jnp.where(vis, jnp.exp(ls_new), 0.0) * vnew_ref[...]

    ls, lk = _log_sigmoid_pair(z_ref[...] + bias)
    suffix = _suffix_matrix(MXU_DIM, BF16)
    run = carry[...]
    after = []
    for c in reversed(range(z_ref.shape[1] // MXU_DIM)):
        lkc = lk[:, c * MXU_DIM:(c + 1) * MXU_DIM]
        after.append(_split_dot(lkc, suffix) + run)
        run = run + jnp.sum(lkc, axis=-1, keepdims=True)
    carry[...] = run
    w_ref[...] = jnp.exp(ls + jnp.concatenate(after[::-1], axis=1))


def _sb_sample_weights(z, q, k_new, v_new, bias_rows, run_after, *, new_key_visible):
    rows, keys = z.shape
    hd = q.shape[1]
    chunk = SB_SAMPLE_KEY_CHUNK
    nsteps = keys // chunk
    zspec = pl.BlockSpec((rows, chunk), lambda c: (0, nsteps - 1 - c))
    small = lambda w: pl.BlockSpec((rows, w), lambda c: (0, 0))
    return pl.pallas_call(
        functools.partial(_sb_sample_weights_kernel, new_key_visible=new_key_visible),
        grid=(nsteps,),
        in_specs=[zspec, small(hd), small(hd), small(hd), small(1),
                  pl.BlockSpec(run_after.shape, lambda c: (0,) * run_after.ndim)],
        out_specs=[zspec, small(hd)],
        out_shape=[jax.ShapeDtypeStruct((rows, keys), F32), jax.ShapeDtypeStruct((rows, hd), F32)],
        scratch_shapes=[pltpu.VMEM((rows, 1), F32)],
        compiler_params=_params("arbitrary"),
        name="sb_sample_weights",
    )(z, q, k_new, v_new, bias_rows, run_after)


def _sb_sample_finish_kernel(parts_ref, onew_ref, o_ref):
    lanes = parts_ref.shape[1]
    j = lax.broadcasted_iota(jnp.int32, (lanes, lanes), 0)
    c = lax.broadcasted_iota(jnp.int32, (lanes, lanes), 1)
    group_sum = jnp.where(j // SC_LANES == c, 1.0, 0.0).astype(BF16)
    o_ref[...] = _split_dot(parts_ref[...], group_sum)[:, :o_ref.shape[1]] + onew_ref[...]


def _sb_sample_finish(parts, o_new):
    rows, lanes = parts.shape
    cols = o_new.shape[1]
    return pl.pallas_call(
        _sb_sample_finish_kernel,
        grid=(1,),
        in_specs=[pl.BlockSpec((rows, lanes), lambda i: (0, 0)), pl.BlockSpec((rows, cols), lambda i: (0, 0))],
        out_specs=pl.BlockSpec((rows, cols), lambda i: (0, 0)),
        out_shape=jax.ShapeDtypeStruct((rows, cols), F32),
        compiler_params=_params("arbitrary"),
        name="sb_sample_finish",
    )(parts, o_new)


def _sb_sample(page_table, q, k_new, v_new, bias, cache_k, cache_v, weights_after, *, new_key_visible):
    nb, n_pages = page_table.shape
    n_pool, page, heads, hd = cache_k.shape
    assert nb == SC_CORES * SC_SUBCORES and page % SC_LANES == 0 and hd % SC_DIM_CHUNK == 0
    assert page == SC_DIM_CHUNK * SC_LANES and n_pages % (2 * SC_GATHER_SLABS) == 0
    slabs = lambda c: c.transpose(0, 2, 3, 1).reshape(n_pool * heads, hd, page)
    idx = (page_table[:, None, :] * heads + jnp.arange(heads, dtype=jnp.int32)[None, :, None]).reshape(
        nb, heads * n_pages // SC_GATHER_SLABS, SC_GATHER_SLABS)
    q_lanes = jnp.repeat(q, SC_LANES, axis=-1).reshape(nb, heads * hd // SC_DIM_CHUNK, page)
    z = _sb_sample_scores(slabs(cache_k), idx, q_lanes, heads=heads)
    per_head = lambda a: a.reshape(nb * heads, hd)
    w, o_new = _sb_sample_weights(z.reshape(nb * heads, n_pages * page), per_head(q), per_head(k_new), per_head(v_new),
                                  jnp.tile(bias, nb).reshape(nb * heads, 1), weights_after,
                                  new_key_visible=new_key_visible)
    parts = _sb_sample_values(slabs(cache_v), idx, w.reshape(nb, heads * n_pages, page), heads=heads)
    o = _sb_sample_finish(parts.reshape(nb * heads * hd // SC_DIM_CHUNK, page),
                          o_new.reshape(nb * heads * hd // SC_DIM_CHUNK, SC_DIM_CHUNK))
    return o.reshape(nb, heads * hd), o_new


def _ssm_prep_kernel(are_ref, aim_ref, ldt_ref, bre_ref, bim_ref, bcat_ref, abr_ref, abi_ref):
    dt = jnp.exp(ldt_ref[...])
    lam_re = jnp.minimum(are_ref[...], A_RE_CLIP)
    lam_im = aim_ref[...]
    mag = jnp.exp(lam_re * dt)
    ab_re = mag * jnp.cos(lam_im * dt)
    ab_im = mag * jnp.sin(lam_im * dt)
    den = lam_re * lam_re + lam_im * lam_im
    nr = ab_re - 1.0
    f_re = (nr * lam_re + ab_im * lam_im) / den
    f_im = (ab_im * lam_re - nr * lam_im) / den
    abr_ref[...] = ab_re
    abi_ref[...] = ab_im
    br, bi = bre_ref[...], bim_ref[...]
    gp = br.shape[1]
    bcat_ref[:, :gp] = (f_re * br - f_im * bi).astype(BF16)
    bcat_ref[:, gp:] = (f_re * bi + f_im * br).astype(BF16)


def _ssm_prep(a_re, a_im, log_dt_full, b_re_bd, b_im_bd):
    gc, gp = b_re_bd.shape
    vec = _const_spec((1, gp))
    return pl.pallas_call(
        _ssm_prep_kernel,
        grid=(1,),
        in_specs=[vec, vec, vec, _const_spec((gc, gp)), _const_spec((gc, gp))],
        out_specs=[pl.BlockSpec((gc, 2 * gp), lambda i: (0, 0)),
                   pl.BlockSpec((1, gp), lambda i: (0, 0)), pl.BlockSpec((1, gp), lambda i: (0, 0))],
        out_shape=[jax.ShapeDtypeStruct((gc, 2 * gp), BF16),
                   jax.ShapeDtypeStruct((1, gp), F32), jax.ShapeDtypeStruct((1, gp), F32)],
        compiler_params=_params("arbitrary"),
        name="ssm_prep",
    )(a_re, a_im, log_dt_full, b_re_bd, b_im_bd)


def _gelu_tanh(y):
    return 0.5 * y * (1.0 + jnp.tanh(math.sqrt(2.0 / math.pi) * (y + 0.044715 * (y * y * y))))


def _ssm_kernel(*refs, steps, nb, has_init):
    (u_ref, bcat_ref, cre_ref, cim_ref, abr_ref, abi_ref, d_ref, wglu_ref, bglu_ref), refs = refs[:9], refs[9:]
    if has_init:
        (s0re_ref, s0im_ref), refs = refs[:2], refs[2:]
    o_ref, sfre_ref, sfim_ref, sre, sim, st_re, st_im = refs
    c = pl.program_id(0)
    gp = sre.shape[1]

    @pl.when(c == 0)
    def _():
        if has_init:
            st_re[...] = s0re_ref[...]
            st_im[...] = s0im_ref[...]
        else:
            st_re[...] = jnp.zeros_like(st_re)
            st_im[...] = jnp.zeros_like(st_im)

    u = u_ref[...]
    ub = u.astype(BF16)
    slabs = u.shape[1] // LANES
    sw = gp // slabs
    for m in range(slabs):
        um = ub[:, m * LANES:(m + 1) * LANES]
        sre[:, m * sw:(m + 1) * sw] = _dot(um, bcat_ref[m * LANES:(m + 1) * LANES, m * sw:(m + 1) * sw])
        sim[:, m * sw:(m + 1) * sw] = _dot(um, bcat_ref[m * LANES:(m + 1) * LANES, gp + m * sw:gp + (m + 1) * sw])
    a_re = jnp.broadcast_to(abr_ref[...], (nb, gp))
    a_im = jnp.broadcast_to(abi_ref[...], (nb, gp))

    def step(t, state):
        s_re, s_im = state
        rows = pl.ds(pl.multiple_of(t * nb, nb), nb)
        n_re = a_re * s_re - a_im * s_im + sre[rows, :]
        n_im = a_re * s_im + a_im * s_re + sim[rows, :]
        sre[rows, :] = n_re
        sim[rows, :] = n_im
        return n_re, n_im

    s_re, s_im = lax.fori_loop(0, steps, step, (st_re[...], st_im[...]), unroll=min(steps, SSM_SCAN_UNROLL))
    st_re[...] = s_re
    st_im[...] = s_im

    y = jnp.concatenate(
        [_dot(sre[:, m * sw:(m + 1) * sw].astype(BF16), cre_ref[m * sw:(m + 1) * sw, m * LANES:(m + 1) * LANES])
         - _dot(sim[:, m * sw:(m + 1) * sw].astype(BF16), cim_ref[m * sw:(m + 1) * sw, m * LANES:(m + 1) * LANES])
         for m in range(slabs)], axis=1)
    y = y + d_ref[...] * u
    g = _gelu_tanh(y)
    o_ref[...] = g * jax.nn.sigmoid(_dot(g.astype(BF16), wglu_ref[...]) + bglu_ref[...])

    @pl.when(c == pl.num_programs(0) - 1)
    def _():
        sfre_ref[...] = s_re
        sfim_ref[...] = s_im


def _ssm(u_tb, bcat, cre_bd, cim_bd, ab_re, ab_im, d_skip, w_glu, b_glu, s0, *, steps, nb):
    rows, width = u_tb.shape
    gp = ab_re.shape[1]
    chunk = steps * nb
    has_init = s0 is not None
    tile = pl.BlockSpec((chunk, width), lambda c: (c, 0))
    state = _const_spec((nb, gp))
    in_specs = [tile, _const_spec(bcat.shape), _const_spec(cre_bd.shape), _const_spec(cim_bd.shape),
                _const_spec((1, gp)), _const_spec((1, gp)), _const_spec((1, width)),
                _const_spec(w_glu.shape), _const_spec((1, width))]
    args = [u_tb, bcat, cre_bd, cim_bd, ab_re, ab_im, d_skip, w_glu, b_glu]
    if has_init:
        in_specs += [state, state]
        args += list(s0)
    return pl.pallas_call(
        functools.partial(_ssm_kernel, steps=steps, nb=nb, has_init=has_init),
        grid=(rows // chunk,),
        in_specs=in_specs,
        out_specs=[tile, pl.BlockSpec((nb, gp), lambda c: (0, 0)), pl.BlockSpec((nb, gp), lambda c: (0, 0))],
        out_shape=[jax.ShapeDtypeStruct((rows, width), F32),
                   jax.ShapeDtypeStruct((nb, gp), F32), jax.ShapeDtypeStruct((nb, gp), F32)],
        scratch_shapes=[pltpu.VMEM((chunk, gp), F32), pltpu.VMEM((chunk, gp), F32),
                        pltpu.VMEM((nb, gp), F32), pltpu.VMEM((nb, gp), F32)],
        compiler_params=_params("arbitrary"),
        name="ssm",
    )(*args)


def _mem_kv_kernel(m_ref, g_ref, wk_ref, wv_ref, k_ref, v_ref, kb_ref, vb_ref):
    h = _rms(m_ref[...], g_ref[...]).astype(BF16)
    k = _dot(h, wk_ref[...])
    v = _dot(h, wv_ref[...])
    k_ref[...] = k
    v_ref[...] = v
    kb_ref[...] = k.astype(BF16)
    vb_ref[...] = v.astype(BF16)


def _mem_kv(mem, g, wk, wv, *, tm):
    n, d = mem.shape
    row = pl.BlockSpec((tm, d), lambda i: (i, 0))
    return pl.pallas_call(
        _mem_kv_kernel,
        grid=(n // tm,),
        in_specs=[row, _const_spec((1, d)), _const_spec(wk.shape), _const_spec(wv.shape)],
        out_specs=[row] * 4,
        out_shape=[jax.ShapeDtypeStruct((n, d), F32)] * 2 + [jax.ShapeDtypeStruct((n, d), BF16)] * 2,
        compiler_params=_params("parallel"),
        name="mem_kv",
    )(mem, g, wk, wv)


def _mix_out_kernel(x1_ref, osb_ref, ossm_ref, gsb_ref, gssm_ref, wout_ref, gx_ref, wxq_ref, x2_ref, qx_ref, *, xscale):
    mixed = jnp.concatenate([_rms(osb_ref[...], gsb_ref[...]), _rms(ossm_ref[...], gssm_ref[...])], axis=-1)
    x2 = x1_ref[...] + _dot(mixed.astype(BF16), wout_ref[...])
    x2_ref[...] = x2
    hx = _rms(x2, gx_ref[...]).astype(BF16)
    qx_ref[...] = (_dot(hx, wxq_ref[...]) * xscale).astype(BF16)


def _mix_out(x1, o_sb, o_ssm, g_sb, g_ssm, w_out, g_x, w_xq, *, tm, ssm_index, xscale):
    n, d = x1.shape
    sbw = o_sb.shape[1]
    ssw = g_ssm.shape[1]
    row = lambda w: pl.BlockSpec((tm, w), lambda i: (i, 0))
    return pl.pallas_call(
        functools.partial(_mix_out_kernel, xscale=xscale),
        grid=(n // tm,),
        in_specs=[row(d), row(sbw), pl.BlockSpec((tm, ssw), ssm_index), _const_spec((1, sbw)), _const_spec((1, ssw)),
                  _const_spec(w_out.shape), _const_spec((1, d)), _const_spec(w_xq.shape)],
        out_specs=[row(d), row(d)],
        out_shape=[jax.ShapeDtypeStruct((n, d), F32), jax.ShapeDtypeStruct((n, d), BF16)],
        compiler_params=_params("parallel"),
        name="mix_out",
    )(x1, o_sb, o_ssm, g_sb, g_ssm, w_out, g_x, w_xq)


def _merge_heads_sc(mem, run_after):
    nb, m, h, d = mem.shape
    rows = MEM_COPY_ROWS
    assert nb == SC_CORES * SC_SUBCORES and m % rows == 0

    @functools.partial(pl.kernel, mesh=_sc_mesh(), out_type=jax.ShapeDtypeStruct((nb, m, h * d), mem.dtype),
                       scratch_types=[pltpu.VMEM((rows, h, d), mem.dtype)])
    def kern(mem_hbm, _order_hbm, out_hbm, buf):
        b = _sc_sequence_index()

        @pl.loop(0, m, step=rows)
        def _(m0):
            pltpu.sync_copy(mem_hbm.at[b, pl.ds(m0, rows)], buf)
            for hh in range(h):
                pltpu.sync_copy(buf.at[:, hh, :], out_hbm.at[b, pl.ds(m0, rows), pl.ds(hh * d, d)])

    return kern(mem, run_after)


def _xattn_kernel(q_ref, k_ref, v_ref, o_ref, *, rows):
    q = q_ref[0]
    if q.shape[0] != rows:
        q = jnp.broadcast_to(q, (rows, q.shape[1]))
    hd = q.shape[1] // X_HEADS

    def mem_head(ref, h):
        return ref[0, :, h * hd:(h + 1) * hd].astype(BF16)

    outs = []
    for h in range(X_HEADS):
        s = _dot_nt(q[:, h * hd:(h + 1) * hd], mem_head(k_ref, h))
        e = jnp.exp(s - jnp.max(s, axis=-1, keepdims=True))
        p = e / jnp.sum(e, axis=-1, keepdims=True)
        outs.append(_dot(p.astype(BF16), mem_head(v_ref, h)))
    o = jnp.concatenate(outs, axis=-1).astype(o_ref.dtype)
    o_ref[0] = o[:o_ref.shape[1]]


def _xattn(q, mem_k, mem_v, *, tq):
    b, l, d = q.shape
    m = mem_k.shape[1]
    rows = max(tq, 16)
    qspec = pl.BlockSpec((1, tq, d), lambda bi, i: (bi, i, 0))
    mspec = pl.BlockSpec((1, m, d), lambda bi, i: (bi, 0, 0))
    return pl.pallas_call(
        functools.partial(_xattn_kernel, rows=rows),
        grid=(b, l // tq),
        in_specs=[qspec, mspec, mspec],
        out_specs=qspec,
        out_shape=jax.ShapeDtypeStruct((b, l, d), BF16),
        compiler_params=_params("parallel", "arbitrary"),
        name="xattn",
    )(q, mem_k, mem_v)


def _out_kernel(x2_ref, ox_ref, wxo_ref, g2_ref, wg_ref, wu_ref, wd_ref, gfin_ref, y_ref):
    x3 = x2_ref[...] + _dot(ox_ref[...], wxo_ref[...])
    x4 = _ffn_half(x3, g2_ref, wg_ref, wu_ref, wd_ref)
    y_ref[...] = _rms(x4, gfin_ref[...])


def _out(x2, ox, w_xo, g2, wg, wu, wd, g_fin, *, tm):
    n, d = x2.shape
    dff = wg.shape[1]
    row = pl.BlockSpec((tm, d), lambda i: (i, 0))
    return pl.pallas_call(
        _out_kernel,
        grid=(n // tm,),
        in_specs=[row, row, _const_spec(w_xo.shape), _const_spec((1, d)), _const_spec((d, dff)),
                  _const_spec((d, dff)), _const_spec((dff, d)), _const_spec((1, d))],
        out_specs=row,
        out_shape=jax.ShapeDtypeStruct((n, d), F32),
        compiler_params=_params("parallel"),
        name="ffn2_out",
    )(x2, ox, w_xo, g2, wg, wu, wd, g_fin)


def _block_diag(x):
    g, r, c = x.shape
    row_block = jnp.arange(g * r)[:, None] // r
    col_block = jnp.arange(g * c)[None, :] // c
    return jnp.where(row_block == col_block, jnp.tile(x.reshape(g * r, c), (1, g)), 0)


def kernel(x_prompt, x_sample, cache_k, cache_v, state_ssm_re, state_ssm_im, cache_mem_k, cache_mem_v, page_table, mem_prompt, g_ffn1, w_ffn1_gate, w_ffn1_up, w_ffn1_down, g_mix, w_in, sb_bias, ssm_a_re, ssm_a_im, ssm_log_dt, ssm_b_re, ssm_b_im, ssm_c_re, ssm_c_im, ssm_d, w_glu, b_glu, g_sb_out, g_ssm_out, w_out, g_xattn, g_mem, w_xq, w_xk, w_xv, w_xo, g_ffn2, w_ffn2_gate, w_ffn2_up, w_ffn2_down, g_final):
    depth = w_in.shape[0]
    assert depth == 1, "single-layer step"
    bsz, seq, d = x_prompt.shape
    dec_b, dec_seq, _ = x_sample.shape
    assert dec_seq == 1
    n_pool, page, sb_heads, hd = cache_k.shape[1:]
    assert hd == SB_HEAD_DIM
    sbw = sb_heads * hd
    groups, states = ssm_a_re.shape[1:]
    gp = groups * states
    ssw = groups * SSM_GROUP
    n_mem = mem_prompt.shape[1]
    past_len = page_table.shape[1] * page
    xscale = (d // X_HEADS) ** -0.5

    bf = lambda w: w[0].astype(BF16)
    vec = lambda g: g.reshape(1, -1)
    ffn1 = (vec(g_ffn1[0]), bf(w_ffn1_gate), bf(w_ffn1_up), bf(w_ffn1_down))
    ffn2 = (vec(g_ffn2[0]), bf(w_ffn2_gate), bf(w_ffn2_up), bf(w_ffn2_down))
    w_in_b, w_out_b, w_xq_b, w_xo_b = bf(w_in), bf(w_out), bf(w_xq), bf(w_xo)

    bcat, ab_re, ab_im = _ssm_prep(
        ssm_a_re[0].reshape(1, gp), ssm_a_im[0].reshape(1, gp),
        jnp.repeat(ssm_log_dt[0], states).reshape(1, gp),
        _block_diag(ssm_b_re[0].transpose(0, 2, 1)), _block_diag(ssm_b_im[0].transpose(0, 2, 1)))
    cre_bd = _block_diag(ssm_c_re[0].transpose(0, 2, 1)).astype(BF16)
    cim_bd = _block_diag(ssm_c_im[0].transpose(0, 2, 1)).astype(BF16)
    ssm_w = (bcat, cre_bd, cim_bd, ab_re, ab_im, vec(ssm_d[0]), bf(w_glu), vec(b_glu[0]))

    row0 = lambda i: (0, 0)
    x1s, k_s, v_s, qbs, _, _, u_s = _in_proj(
        x_sample.reshape(dec_b, d), *ffn1, vec(g_mix[0]), w_in_b,
        tm=dec_b, sb_width=sbw, u_shape=(dec_b, ssw), u_index=row0)

    tm = PROMPT_TOKEN_TILE
    tiles = seq // tm
    tb_index = lambda i: (i % tiles, i // tiles)
    n = bsz * seq
    x1, k_p, v_p, qb, kb, vb, u_tb = _in_proj(
        x_prompt.reshape(n, d), *ffn1, vec(g_mix[0]), w_in_b,
        tm=tm, sb_width=sbw, u_shape=(seq, bsz * ssw), u_index=tb_index, kv_time_tiles=tiles)
    o_sb = _sb_prompt(sb_bias[0], qb.reshape(bsz, seq, sbw), kb.reshape(bsz, seq, sbw), vb.reshape(bsz, seq, sbw),
                      blk=SB_BLOCK)
    o_ssm, sre_p, sim_p = _ssm(u_tb.reshape(seq * bsz, ssw), *ssm_w, None, steps=SSM_CHUNK_STEPS, nb=bsz)
    o_sb_s, weights_done = _sb_sample(page_table, qbs.astype(F32), k_s, v_s, sb_bias[0], cache_k[0], cache_v[0], sre_p,
                                      new_key_visible=bool(past_len < past_len))
    mk_p, mv_p, mkb, mvb = _mem_kv(mem_prompt.reshape(bsz * n_mem, d), vec(g_mem[0]), bf(w_xk), bf(w_xv), tm=tm)
    x2, qx = _mix_out(x1, o_sb.reshape(n, sbw), o_ssm.reshape(seq, bsz * ssw), vec(g_sb_out[0]), vec(g_ssm_out[0]),
                      w_out_b, vec(g_xattn[0]), w_xq_b, tm=tm, ssm_index=tb_index, xscale=xscale)
    ox = _xattn(qx.reshape(bsz, seq, d), mkb.reshape(bsz, n_mem, d), mvb.reshape(bsz, n_mem, d), tq=XATTN_Q_TILE)
    y_p = _out(x2, ox.reshape(n, d), w_xo_b, *ffn2, vec(g_final), tm=tm)

    o_ssm_s, sre_s, sim_s = _ssm(u_s, *ssm_w, (state_ssm_re[0].reshape(dec_b, gp), state_ssm_im[0].reshape(dec_b, gp)),
                                 steps=1, nb=dec_b)
    x2s, qxs = _mix_out(x1s, o_sb_s, o_ssm_s, vec(g_sb_out[0]), vec(g_ssm_out[0]),
                        w_out_b, vec(g_xattn[0]), w_xq_b, tm=dec_b, ssm_index=row0, xscale=xscale)
    mem_s = lambda c: _merge_heads_sc(c[0], weights_done)
    oxs = _xattn(qxs.reshape(dec_b, 1, d), mem_s(cache_mem_k), mem_s(cache_mem_v), tq=1)
    y_s = _out(x2s, oxs.reshape(dec_b, d), w_xo_b, *ffn2, vec(g_final), tm=dec_b)

    kv = lambda a, b_, l_: a.reshape(1, b_, l_, sb_heads, hd)
    kv_t = lambda a: a.reshape(1, bsz, sb_heads, hd, seq).transpose(0, 1, 4, 2, 3)
    st = lambda a, b_: a.reshape(1, b_, groups, states)
    mem = lambda a: a.reshape(1, bsz, n_mem, X_HEADS, d // X_HEADS)
    return (y_p.reshape(bsz, seq, d), y_s.reshape(dec_b, 1, d),
            kv_t(k_p), kv_t(v_p), kv(k_s, dec_b, 1), kv(v_s, dec_b, 1),
            st(sre_p, bsz), st(sim_p, bsz), st(sre_s, dec_b), st(sim_s, dec_b),
            mem(mk_p), mem(mv_p))
```
